```python
import math
import jax, jax.numpy as jnp
from jax import lax
import numpy as np

D_MODEL = 4096
BATCH = 2
SEQ = 8192
DEPTH = 2

N_META = 16
POOL_WINDOWS = (2, 4, 8, 16)
N_POOL_GROUPS = len(POOL_WINDOWS)
POOL_GROUP = D_MODEL // N_POOL_GROUPS
HEAD_DIM = 128
N_HEADS = D_MODEL // (2 * HEAD_DIM)
V_HEAD_DIM = 2 * HEAD_DIM
D_QK = N_HEADS * 2 * HEAD_DIM
D_V = N_HEADS * V_HEAD_DIM
Q_BLOCK = 128
D_FF = 14336 * D_MODEL // 4096
N_EXPERTS = 8
TOP_K = 2
D_FF_EXPERT = D_FF // 4
RMS_EPS = 1e-6
SUBLN_EPS = 1e-5
MASK_VALUE = -1e30

kernel_name = "hybrid_pool_diffattn_moe_trunk"


def rms_norm(x, g, eps=RMS_EPS):
    xf = x.astype(jnp.float32)
    y = xf * lax.rsqrt(jnp.mean(xf * xf, axis=-1, keepdims=True) + eps)
    return (y * g.astype(jnp.float32)).astype(x.dtype)


def causal_window_mean(h, w):
    L = h.shape[1]
    cs = jnp.cumsum(h.astype(jnp.float32), axis=1)
    cs_shift = jnp.pad(cs, ((0, 0), (w, 0), (0, 0)))[:, :L]
    cnt = jnp.minimum(jnp.arange(L) + 1, w).astype(jnp.float32)
    return ((cs - cs_shift) / cnt[None, :, None]).astype(h.dtype)


def pool_mixer(u, pool_w, pool_scale):
    B, L, _ = u.shape
    g = u.reshape(B, L, N_POOL_GROUPS, POOL_GROUP)
    pooled = jnp.stack([causal_window_mean(g[:, :, i], w) for i, w in enumerate(POOL_WINDOWS)], axis=2) - g
    out = jnp.einsum('blgc,gcd->blgd', pooled, pool_w).reshape(B, L, D_MODEL)
    return out * pool_scale


def diff_lambda_init(layer_idx):
    return 0.8 - 0.6 * math.exp(-0.3 * layer_idx)


def diff_attention(u, w_qkv, q_norm, k_norm, lq1, lk1, lq2, lk2, subln, w_o, lambda_init):
    B, L, _ = u.shape
    qkv = u @ w_qkv
    q = qkv[..., :D_QK].reshape(B, L, N_HEADS, 2, HEAD_DIM)
    k = qkv[..., D_QK:2 * D_QK].reshape(B, L, N_HEADS, 2, HEAD_DIM)
    v = qkv[..., 2 * D_QK:].reshape(B, L, N_HEADS, V_HEAD_DIM)
    q = rms_norm(q, q_norm)
    k = rms_norm(k, k_norm)
    lam = (jnp.exp(jnp.sum(lq1.astype(jnp.float32) * lk1.astype(jnp.float32)))
           - jnp.exp(jnp.sum(lq2.astype(jnp.float32) * lk2.astype(jnp.float32)))
           + lambda_init)
    pad_front = (-N_META) % Q_BLOCK
    q = jnp.pad(q, ((0, 0), (pad_front, 0), (0, 0), (0, 0), (0, 0)))
    k = jnp.pad(k, ((0, 0), (pad_front, 0), (0, 0), (0, 0), (0, 0)))
    v = jnp.pad(v, ((0, 0), (pad_front, 0), (0, 0), (0, 0)))
    Lp = L + pad_front
    n_blocks = Lp // Q_BLOCK
    kpos = jnp.arange(Lp)
    scale = HEAD_DIM ** -0.5

    def one_block(qb):
        start = qb * Q_BLOCK
        qblk = lax.dynamic_slice_in_dim(q, start, Q_BLOCK, axis=1)
        s = jnp.einsum('bqhmd,bkhmd->bhmqk', qblk, k, preferred_element_type=jnp.float32) * scale
        qpos = start + jnp.arange(Q_BLOCK)
        mask = (kpos[None, :] <= qpos[:, None]) & (kpos[None, :] >= pad_front)
        p = jax.nn.softmax(jnp.where(mask, s, MASK_VALUE), axis=-1)
        a = p[:, :, 0] - lam * p[:, :, 1]
        return jnp.einsum('bhqk,bkhe->bqhe', a.astype(v.dtype), v)

    o = lax.map(one_block, jnp.arange(n_blocks))
    o = jnp.moveaxis(o, 0, 1).reshape(B, Lp, N_HEADS, V_HEAD_DIM)[:, pad_front:]
    o = rms_norm(o, subln, SUBLN_EPS) * (1.0 - lambda_init)
    return o.reshape(B, L, D_V) @ w_o


def swiglu(u, w_gate, w_up, w_down):
    return (jax.nn.silu(u @ w_gate) * (u @ w_up)) @ w_down


def moe_swiglu(u, router, e_gate, e_up, e_down):
    B, L, D = u.shape
    t = u.reshape(B * L, D)
    logits = (t @ router).astype(jnp.float32)
    top_v, top_i = lax.top_k(logits, TOP_K)
    gates = jax.nn.softmax(top_v, axis=-1)
    comb = jnp.sum(jax.nn.one_hot(top_i, N_EXPERTS, dtype=jnp.float32) * gates[..., None], axis=1)
    out = jnp.zeros((B * L, D), jnp.float32)
    for e in range(N_EXPERTS):
        out = out + comb[:, e:e + 1] * swiglu(t, e_gate[e], e_up[e], e_down[e]).astype(jnp.float32)
    return out.astype(u.dtype).reshape(B, L, D)


def setup_inputs(seed: int = 0) -> dict:
    key = jax.random.key(seed)
    ks = jax.random.split(key, 24)
    n_even = (DEPTH + 1) // 2
    n_odd = DEPTH // 2
    f32 = jnp.float32

    def nrm(k, shape, s):
        return jax.random.normal(k, shape, f32) * s

    return {
        "x": nrm(ks[0], (BATCH, SEQ, D_MODEL), 1.0),
        "meta_tokens": nrm(ks[1], (N_META, D_MODEL), 1.0),
        "norm_mix": 1.0 + nrm(ks[2], (DEPTH, D_MODEL), 0.02),
        "norm_ffn": 1.0 + nrm(ks[3], (DEPTH, D_MODEL), 0.02),
        "pool_w": nrm(ks[4], (n_even, N_POOL_GROUPS, POOL_GROUP, POOL_GROUP), POOL_GROUP ** -0.5),
        "pool_scale": 1.0 + nrm(ks[5], (n_even, D_MODEL), 0.02),
        "ffn_w_gate": nrm(ks[6], (n_even, D_MODEL, D_FF), D_MODEL ** -0.5),
        "ffn_w_up": nrm(ks[7], (n_even, D_MODEL, D_FF), D_MODEL ** -0.5),
        "ffn_w_down": nrm(ks[8], (n_even, D_FF, D_MODEL), D_FF ** -0.5),
        "w_qkv": nrm(ks[9], (n_odd, D_MODEL, 2 * D_QK + D_V), D_MODEL ** -0.5),
        "q_norm": 1.0 + nrm(ks[10], (n_odd, HEAD_DIM), 0.02),
        "k_norm": 1.0 + nrm(ks[11], (n_odd, HEAD_DIM), 0.02),
        "lambda_q1": nrm(ks[12], (n_odd, HEAD_DIM), 0.1),
        "lambda_k1": nrm(ks[13], (n_odd, HEAD_DIM), 0.1),
        "lambda_q2": nrm(ks[14], (n_odd, HEAD_DIM), 0.1),
        "lambda_k2": nrm(ks[15], (n_odd, HEAD_DIM), 0.1),
        "subln": 1.0 + nrm(ks[16], (n_odd, V_HEAD_DIM), 0.02),
        "w_o": nrm(ks[17], (n_odd, D_V, D_MODEL), D_V ** -0.5),
        "router": nrm(ks[18], (n_odd, D_MODEL, N_EXPERTS), D_MODEL ** -0.5),
        "exp_w_gate": nrm(ks[19], (n_odd, N_EXPERTS, D_MODEL, D_FF_EXPERT), D_MODEL ** -0.5),
        "exp_w_up": nrm(ks[20], (n_odd, N_EXPERTS, D_MODEL, D_FF_EXPERT), D_MODEL ** -0.5),
        "exp_w_down": nrm(ks[21], (n_odd, N_EXPERTS, D_FF_EXPERT, D_MODEL), D_FF_EXPERT ** -0.5),
    }


def reference(x, meta_tokens, norm_mix, norm_ffn, pool_w, pool_scale, ffn_w_gate, ffn_w_up, ffn_w_down,
              w_qkv, q_norm, k_norm, lambda_q1, lambda_k1, lambda_q2, lambda_k2, subln, w_o,
              router, exp_w_gate, exp_w_up, exp_w_down):
    B = x.shape[0]
    meta = jnp.broadcast_to(meta_tokens.astype(x.dtype)[None], (B, N_META, D_MODEL))
    h = jnp.concatenate([meta, x], axis=1)
    for i in range(DEPTH):
        j = i // 2
        u = rms_norm(h, norm_mix[i])
        if i % 2 == 0:
            h = h + pool_mixer(u, pool_w[j], pool_scale[j])
        else:
            h = h + diff_attention(u, w_qkv[j], q_norm[j], k_norm[j], lambda_q1[j], lambda_k1[j],
                                   lambda_q2[j], lambda_k2[j], subln[j], w_o[j], diff_lambda_init(i))
        u = rms_norm(h, norm_ffn[i])
        if i % 2 == 0:
            h = h + swiglu(u, ffn_w_gate[j], ffn_w_up[j], ffn_w_down[j])
        else:
            h = h + moe_swiglu(u, router[j], exp_w_gate[j], exp_w_up[j], exp_w_down[j])
    return h[:, N_META:]
```

```python
import functools
import math

import jax
import jax.numpy as jnp
from jax import lax
from jax.experimental import pallas as pl
from jax.experimental.pallas import tpu as pltpu

POOL_WINDOWS = (2, 4, 8, 16)
HEAD_DIM = 128
TOP_K = 2
RMS_EPS = 1e-6
SUBLN_EPS = 1e-5
MASK_VALUE = -1e30

V7X_LANES = 128
V7X_VMEM_BYTES = 64 * 1024 * 1024
V7X_VMEM_RESERVE_BYTES = 6 * 1024 * 1024
BF16_SUBLANES = 16

F32 = jnp.float32
BF16 = jnp.bfloat16


def _pick(dim, target, mult):
    for b in range(min(dim, target), 0, -1):
        if dim % b == 0 and b % mult == 0:
            return b
    return dim


def _nbytes(shape, dtype):
    return math.prod(shape) * jnp.dtype(dtype).itemsize


def _params(semantics, pipelined, scratch=()):
    est = 2 * sum(_nbytes(s, d) for s, d in pipelined) + sum(_nbytes(s, d) for s, d in scratch)
    limit = min(V7X_VMEM_BYTES - V7X_VMEM_RESERVE_BYTES, est + 16 * 1024 * 1024)
    return pltpu.CompilerParams(dimension_semantics=semantics, vmem_limit_bytes=limit)


def _rms(v, eps):
    return v * lax.rsqrt(jnp.mean(v * v, axis=-1, keepdims=True) + eps)


def _pool_kernel(x_ref, prev_ref, first_ref, gm_ref, pw_ref, ps_ref, gf_ref, h_ref, u_ref, ext_ref,
                 *, windows, halo, pos0):
    i = pl.program_id(1)
    tm = x_ref.shape[1]
    c = pw_ref.shape[1]
    x = x_ref[0]
    gm = gm_ref[...]
    before = jnp.where(i == 0, first_ref[...], prev_ref[0])
    ext_ref[0:halo] = _rms(before, RMS_EPS) * gm
    ext_ref[halo:] = _rms(x, RMS_EPS) * gm
    pos = pos0 + i * tm + lax.broadcasted_iota(jnp.int32, (tm, 1), 0)
    for g, w in enumerate(windows):
        cs = slice(g * c, (g + 1) * c)
        u = ext_ref[halo:halo + tm, cs]
        acc = u
        for s in range(1, w):
            acc = acc + ext_ref[halo - s:halo - s + tm, cs]
        inv_cnt = 1.0 / jnp.minimum(pos + 1, w).astype(F32)
        pooled = acc * inv_cnt - u
        out = jnp.dot(pooled.astype(BF16), pw_ref[g], preferred_element_type=F32)
        h_ref[0, :, cs] = x[:, cs] + out * ps_ref[:, cs]
    h = h_ref[0]
    u_ref[0] = (_rms(h, RMS_EPS) * gf_ref[...]).astype(BF16)


def _pool_layer(x, first_rows, g_mix, pool_w, pool_scale, g_ffn, pos0):
    b, l, d = x.shape
    halo = max(POOL_WINDOWS)
    assert first_rows.shape == (halo, d)
    tm = _pick(l, 256, halo)
    nb = tm // halo
    g, c, _ = pool_w.shape
    kern = functools.partial(_pool_kernel, windows=POOL_WINDOWS, halo=halo, pos0=pos0)
    row = lambda bi, i: (bi, i, 0)
    const2 = lambda bi, i: (0, 0)
    blocks = [((1, tm, d), F32), ((1, halo, d), F32), ((halo, d), F32), ((g, c, c), BF16),
              ((1, tm, d), F32), ((1, tm, d), BF16)]
    scratch = [((halo + tm, d), F32)]
    return pl.pallas_call(
        kern,
        grid=(b, l // tm),
        in_specs=[
            pl.BlockSpec((1, tm, d), row),
            pl.BlockSpec((1, halo, d), lambda bi, i: (bi, jnp.maximum(i * nb - 1, 0), 0)),
            pl.BlockSpec((halo, d), const2),
            pl.BlockSpec((1, d), const2),
            pl.BlockSpec((g, c, c), lambda bi, i: (0, 0, 0)),
            pl.BlockSpec((1, d), const2),
            pl.BlockSpec((1, d), const2),
        ],
        out_specs=[pl.BlockSpec((1, tm, d), row), pl.BlockSpec((1, tm, d), row)],
        out_shape=[jax.ShapeDtypeStruct((b, l, d), F32), jax.ShapeDtypeStruct((b, l, d), BF16)],
        scratch_shapes=[pltpu.VMEM(s, t) for s, t in scratch],
        compiler_params=_params(("arbitrary", "arbitrary"), blocks, scratch),
        name="pool_layer",
    )(x, x, first_rows, g_mix, pool_w, pool_scale, g_ffn)


def _gateup_kernel(x_ref, wg_ref, wu_ref, o_ref):
    x = x_ref[...]
    g = jnp.dot(x, wg_ref[...], preferred_element_type=F32)
    u = jnp.dot(x, wu_ref[...], preferred_element_type=F32)
    o_ref[...] = (g * jax.nn.sigmoid(g) * u).astype(o_ref.dtype)


def _gateup(x, wg, wu):
    m, k = x.shape
    n = wg.shape[1]
    bm = _pick(m, 1024, BF16_SUBLANES)
    bn = _pick(n, 512, V7X_LANES)
    blocks = [((bm, k), BF16), ((k, bn), BF16), ((k, bn), BF16), ((bm, bn), BF16)]
    return pl.pallas_call(
        _gateup_kernel,
        grid=(m // bm, n // bn),
        in_specs=[
            pl.BlockSpec((bm, k), lambda i, j: (i, 0)),
            pl.BlockSpec((k, bn), lambda i, j: (0, j)),
            pl.BlockSpec((k, bn), lambda i, j: (0, j)),
        ],
        out_specs=pl.BlockSpec((bm, bn), lambda i, j: (i, j)),
        out_shape=jax.ShapeDtypeStruct((m, n), BF16),
        compiler_params=_params(("arbitrary", "arbitrary"), blocks),
        name="ffn_gateup",
    )(x, wg, wu)


def _matmul_residual_kernel(a_ref, w_ref, r_ref, o_ref):
    kk = pl.program_id(2)
    part = jnp.dot(a_ref[...], w_ref[...], preferred_element_type=F32)

    @pl.when(kk == 0)
    def _():
        o_ref[...] = r_ref[...] + part

    @pl.when(kk != 0)
    def _():
        o_ref[...] += part


def _matmul_residual(a, w, r, bk_target):
    m, k = a.shape
    n = w.shape[1]
    bm = _pick(m, 1024, BF16_SUBLANES)
    bn = _pick(n, 1024, V7X_LANES)
    bk = _pick(k, bk_target, V7X_LANES)
    blocks = [((bm, bk), BF16), ((bk, bn), BF16), ((bm, bn), F32), ((bm, bn), F32)]
    return pl.pallas_call(
        _matmul_residual_kernel,
        grid=(m // bm, n // bn, k // bk),
        in_specs=[
            pl.BlockSpec((bm, bk), lambda i, j, kk: (i, kk)),
            pl.BlockSpec((bk, bn), lambda i, j, kk: (kk, j)),
            pl.BlockSpec((bm, bn), lambda i, j, kk: (i, j)),
        ],
        out_specs=pl.BlockSpec((bm, bn), lambda i, j, kk: (i, j)),
        out_shape=jax.ShapeDtypeStruct((m, n), F32),
        compiler_params=_params(("arbitrary", "arbitrary", "arbitrary"), blocks),
        name="matmul_residual",
    )(a, w, r)


def _qkv_kernel(x_ref, g_ref, w_ref, qn_ref, kn_ref, o_ref, xn_ref, *, n_q_tiles, hd, q_scale):
    j = pl.program_id(1)

    @pl.when(j == 0)
    def _():
        xn_ref[...] = (_rms(x_ref[...], RMS_EPS) * g_ref[...]).astype(BF16)

    acc = jnp.dot(xn_ref[...], w_ref[...], preferred_element_type=F32)

    @pl.when(j < 2 * n_q_tiles)
    def _():
        gain = jnp.where(j < n_q_tiles, qn_ref[...] * q_scale, kn_ref[...])
        for c in range(acc.shape[1] // hd):
            cs = slice(c * hd, (c + 1) * hd)
            o_ref[:, cs] = (_rms(acc[:, cs], RMS_EPS) * gain).astype(o_ref.dtype)

    @pl.when(j >= 2 * n_q_tiles)
    def _():
        o_ref[...] = acc.astype(o_ref.dtype)


def _qkv_proj(h, g, w, q_norm, k_norm):
    m, d = h.shape
    n = w.shape[1]
    bm = _pick(m, 512, BF16_SUBLANES)
    bn = _pick(d, 1024, HEAD_DIM)
    kern = functools.partial(_qkv_kernel, n_q_tiles=d // bn, hd=HEAD_DIM, q_scale=HEAD_DIM ** -0.5)
    blocks = [((bm, d), F32), ((d, bn), BF16), ((bm, bn), BF16)]
    scratch = [((bm, d), BF16)]
    return pl.pallas_call(
        kern,
        grid=(m // bm, n // bn),
        in_specs=[
            pl.BlockSpec((bm, d), lambda i, j: (i, 0)),
            pl.BlockSpec((1, d), lambda i, j: (0, 0)),
            pl.BlockSpec((d, bn), lambda i, j: (0, j)),
            pl.BlockSpec((1, HEAD_DIM), lambda i, j: (0, 0)),
            pl.BlockSpec((1, HEAD_DIM), lambda i, j: (0, 0)),
        ],
        out_specs=pl.BlockSpec((bm, bn), lambda i, j: (i, j)),
        out_shape=jax.ShapeDtypeStruct((m, n), BF16),
        scratch_shapes=[pltpu.VMEM(s, t) for s, t in scratch],
        compiler_params=_params(("arbitrary", "arbitrary"), blocks, scratch),
        name="qkv_proj",
    )(h, g, w, q_norm, k_norm)


def _attn_kernel(q_ref, k_ref, v_ref, km_ref, vm_ref, lq1_ref, lk1_ref, lq2_ref, lk2_ref, sub_ref,
                 o_ref, acc_ref, m_ref, l_ref, *, tk, hd, n_meta, lambda_init):
    qi = pl.program_id(2)
    tq = q_ref.shape[0]
    nt = (((1,), (1,)), ((), ()))

    def update(mp, k, v, mask, first):
        q = q_ref[:, mp * hd:(mp + 1) * hd]
        s = lax.dot_general(q, k, nt, preferred_element_type=F32)
        if mask is not None:
            s = jnp.where(mask, s, MASK_VALUE)
        row_max = jnp.max(s, axis=1, keepdims=True)
        if first:
            m_new = row_max
        else:
            m_old = m_ref[mp][:, :1]
            m_new = jnp.maximum(m_old, row_max)
        p = jnp.exp(s - m_new)
        row_sum = jnp.sum(p, axis=1, keepdims=True)
        pv = jnp.dot(p.astype(BF16), v, preferred_element_type=F32)
        if first:
            l_new = row_sum
            acc_ref[mp] = pv
        else:
            alpha = jnp.exp(m_old - m_new)
            l_new = alpha * l_ref[mp][:, :1] + row_sum
            acc_ref[mp] = alpha * acc_ref[mp] + pv
        m_ref[mp] = jnp.broadcast_to(m_new, (tq, V7X_LANES))
        l_ref[mp] = jnp.broadcast_to(l_new, (tq, V7X_LANES))

    meta_mask = lax.broadcasted_iota(jnp.int32, (tq, km_ref.shape[0]), 1) < n_meta
    for mp in range(2):
        update(mp, km_ref[:, mp * hd:(mp + 1) * hd], vm_ref[...], meta_mask, True)

    n_full = (qi * tq) // tk

    def full_tile(kt, carry):
        k0 = pl.multiple_of(kt * tk, tk)
        for mp in range(2):
            update(mp, k_ref[pl.ds(k0, tk), mp * hd:(mp + 1) * hd], v_ref[pl.ds(k0, tk), :], None, False)
        return carry

    lax.fori_loop(0, n_full, full_tile, 0)

    row = lax.broadcasted_iota(jnp.int32, (tq, tk), 0)
    col = lax.broadcasted_iota(jnp.int32, (tq, tk), 1)
    for dt in range(tq // tk):
        k0 = pl.multiple_of(qi * tq + dt * tk, tk)
        diag_mask = col + dt * tk <= row
        for mp in range(2):
            update(mp, k_ref[pl.ds(k0, tk), mp * hd:(mp + 1) * hd], v_ref[pl.ds(k0, tk), :], diag_mask, False)

    lam = (jnp.exp(jnp.sum(lq1_ref[...] * lk1_ref[...], axis=-1, keepdims=True))
           - jnp.exp(jnp.sum(lq2_ref[...] * lk2_ref[...], axis=-1, keepdims=True)) + lambda_init)
    o1 = acc_ref[0] / l_ref[0][:, :1]
    o2 = acc_ref[1] / l_ref[1][:, :1]
    o = o1 - lam * o2
    o = _rms(o, SUBLN_EPS) * sub_ref[...] * (1.0 - lambda_init)
    o_ref[...] = o.astype(o_ref.dtype)


def _diff_attention(qkv, qkv_meta, lq1, lk1, lq2, lk2, subln, batch, n_meta, lambda_init):
    t, n3 = qkv.shape
    d = n3 // 3
    l = t // batch
    hw = 2 * HEAD_DIM
    heads = d // hw
    tq = _pick(l, 512, V7X_LANES)
    tk = _pick(tq, 512, V7X_LANES)
    nq = l // tq
    mrows = qkv_meta.shape[0]
    kern = functools.partial(_attn_kernel, tk=tk, hd=HEAD_DIM, n_meta=n_meta, lambda_init=lambda_init)
    vec = pl.BlockSpec((1, HEAD_DIM), lambda b, h, i: (0, 0))
    blocks = [((tq, hw), BF16), ((l, hw), BF16), ((l, hw), BF16), ((mrows, hw), BF16), ((mrows, hw), BF16),
              ((tq, hw), BF16)]
    scratch = [((2, tq, hw), F32), ((2, tq, V7X_LANES), F32), ((2, tq, V7X_LANES), F32)]
    return pl.pallas_call(
        kern,
        grid=(batch, heads, nq),
        in_specs=[
            pl.BlockSpec((tq, hw), lambda b, h, i: (b * nq + i, h)),
            pl.BlockSpec((l, hw), lambda b, h, i: (b, heads + h)),
            pl.BlockSpec((l, hw), lambda b, h, i: (b, 2 * heads + h)),
            pl.BlockSpec((mrows, hw), lambda b, h, i: (0, heads + h)),
            pl.BlockSpec((mrows, hw), lambda b, h, i: (0, 2 * heads + h)),
            vec, vec, vec, vec,
            pl.BlockSpec((1, hw), lambda b, h, i: (0, 0)),
        ],
        out_specs=pl.BlockSpec((tq, hw), lambda b, h, i: (b * nq + i, h)),
        out_shape=jax.ShapeDtypeStruct((t, d), BF16),
        scratch_shapes=[pltpu.VMEM(s, ty) for s, ty in scratch],
        compiler_params=_params(("arbitrary", "arbitrary", "arbitrary"), blocks, scratch),
        name="diff_attention",
    )(qkv, qkv, qkv, qkv_meta, qkv_meta, lq1, lk1, lq2, lk2, subln)


def _router_kernel(x_ref, g_ref, r_ref, up_ref, rt_ref, *, n_exp):
    x = x_ref[...]
    u = _rms(x, RMS_EPS) * g_ref[...]
    logits = jnp.dot(u, r_ref[...], preferred_element_type=F32, precision=lax.Precision.HIGHEST)
    lane = lax.broadcasted_iota(jnp.int32, logits.shape, 1).astype(F32)
    lg = jnp.where(lane < n_exp, logits, -jnp.inf)
    m1 = jnp.max(lg, axis=1, keepdims=True)
    i1 = jnp.min(jnp.where(lg == m1, lane, float(V7X_LANES)), axis=1, keepdims=True)
    lg2 = jnp.where(lane == i1, -jnp.inf, lg)
    m2 = jnp.max(lg2, axis=1, keepdims=True)
    i2 = jnp.min(jnp.where(lg2 == m2, lane, float(V7X_LANES)), axis=1, keepdims=True)
    e = jnp.exp(m2 - m1)
    g1 = 1.0 / (1.0 + e)
    g2 = e / (1.0 + e)
    rt = jnp.where(lane == 0, i1, jnp.where(lane == 1, i2, jnp.where(lane == 2, g1, jnp.where(lane == 3, g2, 0.0))))
    rt_ref[...] = rt
    bits = lax.bitcast_convert_type(u.astype(BF16).astype(F32), jnp.uint32)
    d2 = x.shape[1] // 2
    up_ref[...] = (bits[:, :d2] >> 16) | (bits[:, d2:] & jnp.uint32(0xFFFF0000))


def _router(h, g, router_padded, n_exp):
    m, d = h.shape
    bm = _pick(m, 256, 8)
    kern = functools.partial(_router_kernel, n_exp=n_exp)
    blocks = [((bm, d), F32), ((d, V7X_LANES), F32), ((bm, d // 2), jnp.uint32), ((bm, V7X_LANES), F32)]
    return pl.pallas_call(
        kern,
        grid=(m // bm,),
        in_specs=[
            pl.BlockSpec((bm, d), lambda i: (i, 0)),
            pl.BlockSpec((1, d), lambda i: (0, 0)),
            pl.BlockSpec((d, V7X_LANES), lambda i: (0, 0)),
        ],
        out_specs=[pl.BlockSpec((bm, d // 2), lambda i: (i, 0)), pl.BlockSpec((bm, V7X_LANES), lambda i: (i, 0))],
        out_shape=[jax.ShapeDtypeStruct((m, d // 2), jnp.uint32), jax.ShapeDtypeStruct((m, V7X_LANES), F32)],
        compiler_params=_params(("arbitrary",), blocks),
        name="router",
    )(h, g, router_padded)


def _dispatch_kernel(pos_ref, u_ref, xs_in_ref, xs_ref, sem):
    del xs_in_ref
    bm = u_ref.shape[0]

    def row_copy(r, s):
        return pltpu.make_async_copy(u_ref.at[pl.ds(r, 1)], xs_ref.at[pl.ds(pos_ref[0, 0, 2 * r + s], 1)], sem)

    def start(r, c):
        row_copy(r, 0).start()
        row_copy(r, 1).start()
        return c

    def wait(r, c):
        row_copy(r, 0).wait()
        row_copy(r, 1).wait()
        return c

    lax.fori_loop(0, bm, start, 0)
    lax.fori_loop(0, bm, wait, 0)


def _dispatch(up, pos_tiles, xs_zero):
    m, d2 = up.shape
    bm = pos_tiles.shape[2] // TOP_K
    blocks = [((bm, d2), jnp.uint32)]
    return pl.pallas_call(
        _dispatch_kernel,
        grid=(m // bm,),
        in_specs=[
            pl.BlockSpec((1, 1, TOP_K * bm), lambda i: (i, 0, 0), memory_space=pltpu.SMEM),
            pl.BlockSpec((bm, d2), lambda i: (i, 0)),
            pl.BlockSpec(memory_space=pl.ANY),
        ],
        out_specs=pl.BlockSpec(memory_space=pl.ANY),
        out_shape=jax.ShapeDtypeStruct(xs_zero.shape, xs_zero.dtype),
        scratch_shapes=[pltpu.SemaphoreType.DMA(())],
        input_output_aliases={2: 0},
        compiler_params=_params(("arbitrary",), blocks),
        name="moe_dispatch",
    )(pos_tiles, up, xs_zero)


def _combine_kernel(pos_ref, h_ref, rt_ref, y_ref, o_ref, buf_ref, sem):
    bm = h_ref.shape[0]

    def row_copy(r, s):
        return pltpu.make_async_copy(y_ref.at[pl.ds(pos_ref[0, 0, 2 * r + s], 1)], buf_ref.at[s, pl.ds(r, 1)], sem)

    def start(r, c):
        row_copy(r, 0).start()
        row_copy(r, 1).start()
        return c

    def wait(r, c):
        row_copy(r, 0).wait()
        row_copy(r, 1).wait()
        return c

    lax.fori_loop(0, bm, start, 0)
    lax.fori_loop(0, bm, wait, 0)
    rt = rt_ref[...]
    o_ref[...] = h_ref[...] + rt[:, 2:3] * buf_ref[0] + rt[:, 3:4] * buf_ref[1]


def _combine(h, route, y, pos_tiles):
    m, d = h.shape
    bm = pos_tiles.shape[2] // TOP_K
    blocks = [((bm, d), F32), ((bm, V7X_LANES), F32), ((bm, d), F32)]
    scratch = [((TOP_K, bm, d), F32)]
    return pl.pallas_call(
        _combine_kernel,
        grid=(m // bm,),
        in_specs=[
            pl.BlockSpec((1, 1, TOP_K * bm), lambda i: (i, 0, 0), memory_space=pltpu.SMEM),
            pl.BlockSpec((bm, d), lambda i: (i, 0)),
            pl.BlockSpec((bm, V7X_LANES), lambda i: (i, 0)),
            pl.BlockSpec(memory_space=pl.ANY),
        ],
        out_specs=pl.BlockSpec((bm, d), lambda i: (i, 0)),
        out_shape=jax.ShapeDtypeStruct((m, d), F32),
        scratch_shapes=[pltpu.VMEM(scratch[0][0], F32), pltpu.SemaphoreType.DMA(())],
        compiler_params=_params(("arbitrary",), blocks, scratch),
        name="moe_combine",
    )(pos_tiles, h, route, y)


def _expert_gateup_kernel(te_ref, nu_ref, x_ref, wg_ref, wu_ref, o_ref, xs_ref):
    del te_ref
    i = pl.program_id(0)
    j = pl.program_id(1)
    d2 = x_ref.shape[1]

    @pl.when(i < nu_ref[0])
    def _():
        @pl.when(j == 0)
        def _():
            w = x_ref[...]
            xs_ref[:, :d2] = lax.bitcast_convert_type(w << 16, F32).astype(BF16)
            xs_ref[:, d2:] = lax.bitcast_convert_type(w & jnp.uint32(0xFFFF0000), F32).astype(BF16)

        x = xs_ref[...]
        g = jnp.dot(x, wg_ref[0], preferred_element_type=F32)
        u = jnp.dot(x, wu_ref[0], preferred_element_type=F32)
        o_ref[...] = (g * jax.nn.sigmoid(g) * u).astype(o_ref.dtype)

    @pl.when(i >= nu_ref[0])
    def _():
        o_ref[...] = jnp.zeros_like(o_ref)


def _expert_gateup(xs, wg, wu, tile_expert, n_used, tm):
    p, d2 = xs.shape
    _, d, fe = wg.shape
    bn = _pick(fe, 512, V7X_LANES)
    blocks = [((tm, d2), jnp.uint32), ((1, d, bn), BF16), ((1, d, bn), BF16), ((tm, bn), BF16)]
    scratch = [((tm, d), BF16)]
    grid_spec = pltpu.PrefetchScalarGridSpec(
        num_scalar_prefetch=2,
        grid=(p // tm, fe // bn),
        in_specs=[
            pl.BlockSpec((tm, d2), lambda i, j, te, nu: (i, 0)),
            pl.BlockSpec((1, d, bn), lambda i, j, te, nu: (te[i], 0, j)),
            pl.BlockSpec((1, d, bn), lambda i, j, te, nu: (te[i], 0, j)),
        ],
        out_specs=pl.BlockSpec((tm, bn), lambda i, j, te, nu: (i, j)),
        scratch_shapes=[pltpu.VMEM(s, t) for s, t in scratch],
    )
    return pl.pallas_call(
        _expert_gateup_kernel,
        grid_spec=grid_spec,
        out_shape=jax.ShapeDtypeStruct((p, fe), BF16),
        compiler_params=_params(("arbitrary", "arbitrary"), blocks, scratch),
        name="expert_gateup",
    )(tile_expert, n_used, xs, wg, wu)


def _expert_down_kernel(te_ref, nu_ref, a_ref, w_ref, o_ref):
    del te_ref
    i = pl.program_id(0)

    @pl.when(i < nu_ref[0])
    def _():
        o_ref[...] = jnp.dot(a_ref[...], w_ref[0], preferred_element_type=F32)

    @pl.when(i >= nu_ref[0])
    def _():
        o_ref[...] = jnp.zeros_like(o_ref)


def _expert_down(a, wd, tile_expert, n_used, tm):
    p, fe = a.shape
    d = wd.shape[2]
    bn = _pick(d, 1024, V7X_LANES)
    blocks = [((tm, fe), BF16), ((1, fe, bn), BF16), ((tm, bn), F32)]
    grid_spec = pltpu.PrefetchScalarGridSpec(
        num_scalar_prefetch=2,
        grid=(p // tm, d // bn),
        in_specs=[
            pl.BlockSpec((tm, fe), lambda i, j, te, nu: (i, 0)),
            pl.BlockSpec((1, fe, bn), lambda i, j, te, nu: (te[i], 0, j)),
        ],
        out_specs=pl.BlockSpec((tm, bn), lambda i, j, te, nu: (i, j)),
    )
    return pl.pallas_call(
        _expert_down_kernel,
        grid_spec=grid_spec,
        out_shape=jax.ShapeDtypeStruct((p, d), F32),
        compiler_params=_params(("arbitrary", "arbitrary"), blocks),
        name="expert_down",
    )(tile_expert, n_used, a, wd)


def _routing_tables(route, n_exp, tm, n_tiles):
    experts = route[:, :TOP_K].astype(jnp.int32).reshape(-1)
    onehot = (experts[:, None] == jnp.arange(n_exp, dtype=jnp.int32)[None, :]).astype(jnp.int32)
    csum = jnp.cumsum(onehot, axis=0)
    rank = jnp.sum(csum * onehot, axis=1) - 1
    counts = csum[-1]
    padded = ((counts + tm - 1) // tm) * tm
    ends = jnp.cumsum(padded)
    starts = ends - padded
    pos = jnp.sum(starts[None, :] * onehot, axis=1) + rank
    n_used = (ends[-1] // tm).astype(jnp.int32)
    tile_row = jnp.minimum(jnp.arange(n_tiles, dtype=jnp.int32), n_used - 1) * tm
    tile_expert = jnp.sum((tile_row[:, None] >= ends[None, :]).astype(jnp.int32), axis=1)
    return pos, jnp.minimum(tile_expert, n_exp - 1).astype(jnp.int32), n_used.reshape(1)


def kernel(x, meta_tokens, norm_mix, norm_ffn, pool_w, pool_scale, ffn_w_gate, ffn_w_up, ffn_w_down,
           w_qkv, q_norm, k_norm, lambda_q1, lambda_k1, lambda_q2, lambda_k2, subln, w_o,
           router, exp_w_gate, exp_w_up, exp_w_down):
    batch, seq, d = x.shape
    n_meta = meta_tokens.shape[0]
    n_exp = router.shape[-1]
    assert norm_mix.shape[0] == 2, "one pooling layer followed by one attention layer"
    halo = max(POOL_WINDOWS)
    assert n_meta >= halo and n_meta <= V7X_LANES
    t = batch * seq
    lambda_init = 0.8 - 0.6 * math.exp(-0.3 * 1)

    row = lambda v: v.reshape(1, -1).astype(F32)
    pool_wb = pool_w[0].astype(BF16)
    wg, wu, wd = ffn_w_gate[0].astype(BF16), ffn_w_up[0].astype(BF16), ffn_w_down[0].astype(BF16)
    wqkv, wo = w_qkv[0].astype(BF16), w_o[0].astype(BF16)
    ewg, ewu, ewd = exp_w_gate[0].astype(BF16), exp_w_up[0].astype(BF16), exp_w_down[0].astype(BF16)
    router_padded = jnp.pad(router[0].astype(F32), ((0, 0), (0, V7X_LANES - n_exp)))

    def layer0_and_qkv(tokens, first_rows, pos0):
        b, l, _ = tokens.shape
        h1, u2 = _pool_layer(tokens, first_rows, row(norm_mix[0]), pool_wb, row(pool_scale[0]),
                             row(norm_ffn[0]), pos0)
        a = _gateup(u2.reshape(b * l, d), wg, wu)
        h2 = _matmul_residual(a, wd, h1.reshape(b * l, d), 2048)
        qkv = _qkv_proj(h2, row(norm_mix[1]), wqkv, row(q_norm[0]), row(k_norm[0]))
        return h2, qkv

    meta = meta_tokens.astype(F32)
    _, qkv_meta = layer0_and_qkv(meta[None], jnp.zeros((halo, d), F32), 0)
    qkv_meta = jnp.pad(qkv_meta, ((0, V7X_LANES - n_meta), (0, 0)))
    h2, qkv = layer0_and_qkv(x, meta[n_meta - halo:], n_meta)

    o = _diff_attention(qkv, qkv_meta, row(lambda_q1[0]), row(lambda_k1[0]), row(lambda_q2[0]),
                        row(lambda_k2[0]), row(subln[0]), batch, n_meta, lambda_init)
    h3 = _matmul_residual(o, wo, h2, d)

    up, route = _router(h3, row(norm_ffn[1]), router_padded, n_exp)
    tm = _pick(t, 512, BF16_SUBLANES)
    n_tiles = (TOP_K * t) // tm + n_exp
    pos, tile_expert, n_used = _routing_tables(route, n_exp, tm, n_tiles)
    bm_rows = _pick(t, 128, 8)
    pos_tiles = pos.reshape(t // bm_rows, 1, TOP_K * bm_rows)
    xs = _dispatch(up, pos_tiles, jnp.zeros((n_tiles * tm, d // 2), jnp.uint32))
    a = _expert_gateup(xs, ewg, ewu, tile_expert, n_used, tm)
    y = _expert_down(a, ewd, tile_expert, n_used, tm)
    out = _combine(h3, route, y, pos_tiles)
    return out.reshape(batch, seq, d)
```

```python
import functools
import math

import jax
import jax.numpy as jnp
from jax import lax
from jax.experimental import pallas as pl
from jax.experimental.pallas import tpu as pltpu

POOL_WINDOWS = (2, 4, 8, 16)
HEAD_DIM = 128
TOP_K = 2
RMS_EPS = 1e-6
SUBLN_EPS = 1e-5
MASK_VALUE = -1e30

V7X_LANES = 128
V7X_VMEM_BYTES = 64 * 1024 * 1024
V7X_VMEM_RESERVE_BYTES = 6 * 1024 * 1024
BF16_SUBLANES = 16

F32 = jnp.float32
BF16 = jnp.bfloat16


def _pick(dim, target, mult):
    for b in range(min(dim, target), 0, -1):
        if dim % b == 0 and b % mult == 0:
            return b
    return dim


def _nbytes(shape, dtype):
    return math.prod(shape) * jnp.dtype(dtype).itemsize


def _params(semantics, pipelined, scratch=()):
    est = 2 * sum(_nbytes(s, d) for s, d in pipelined) + sum(_nbytes(s, d) for s, d in scratch)
    limit = min(V7X_VMEM_BYTES - V7X_VMEM_RESERVE_BYTES, est + 16 * 1024 * 1024)
    return pltpu.CompilerParams(dimension_semantics=semantics, vmem_limit_bytes=limit)


def _rms(v, eps):
    return v * lax.rsqrt(jnp.mean(v * v, axis=-1, keepdims=True) + eps)


def _pool_kernel(x_ref, prev_ref, first_ref, gm_ref, pw_ref, ps_ref, gf_ref, h_ref, u_ref, ext_ref,
                 *, windows, halo, pos0):
    i = pl.program_id(1)
    tm = x_ref.shape[1]
    c = pw_ref.shape[1]
    x = x_ref[0]
    gm = gm_ref[...]
    before = jnp.where(i == 0, first_ref[...], prev_ref[0])
    ext_ref[0:halo] = _rms(before, RMS_EPS) * gm
    ext_ref[halo:] = _rms(x, RMS_EPS) * gm
    pos = pos0 + i * tm + lax.broadcasted_iota(jnp.int32, (tm, 1), 0)
    for g, w in enumerate(windows):
        cs = slice(g * c, (g + 1) * c)
        u = ext_ref[halo:halo + tm, cs]
        acc = u
        for s in range(1, w):
            acc = acc + ext_ref[halo - s:halo - s + tm, cs]
        inv_cnt = 1.0 / jnp.minimum(pos + 1, w).astype(F32)
        pooled = acc * inv_cnt - u
        out = jnp.dot(pooled.astype(BF16), pw_ref[g], preferred_element_type=F32)
        h_ref[0, :, cs] = x[:, cs] + out * ps_ref[:, cs]
    h = h_ref[0]
    u_ref[0] = (_rms(h, RMS_EPS) * gf_ref[...]).astype(BF16)


def _pool_layer(x, first_rows, g_mix, pool_w, pool_scale, g_ffn, pos0):
    b, l, d = x.shape
    halo = max(POOL_WINDOWS)
    assert first_rows.shape == (halo, d)
    tm = _pick(l, 256, halo)
    nb = tm // halo
    g, c, _ = pool_w.shape
    kern = functools.partial(_pool_kernel, windows=POOL_WINDOWS, halo=halo, pos0=pos0)
    row = lambda bi, i: (bi, i, 0)
    const2 = lambda bi, i: (0, 0)
    blocks = [((1, tm, d), F32), ((1, halo, d), F32), ((halo, d), F32), ((g, c, c), BF16),
              ((1, tm, d), F32), ((1, tm, d), BF16)]
    scratch = [((halo + tm, d), F32)]
    return pl.pallas_call(
        kern,
        grid=(b, l // tm),
        in_specs=[
            pl.BlockSpec((1, tm, d), row),
            pl.BlockSpec((1, halo, d), lambda bi, i: (bi, jnp.maximum(i * nb - 1, 0), 0)),
            pl.BlockSpec((halo, d), const2),
            pl.BlockSpec((1, d), const2),
            pl.BlockSpec((g, c, c), lambda bi, i: (0, 0, 0)),
            pl.BlockSpec((1, d), const2),
            pl.BlockSpec((1, d), const2),
        ],
        out_specs=[pl.BlockSpec((1, tm, d), row), pl.BlockSpec((1, tm, d), row)],
        out_shape=[jax.ShapeDtypeStruct((b, l, d), F32), jax.ShapeDtypeStruct((b, l, d), BF16)],
        scratch_shapes=[pltpu.VMEM(s, t) for s, t in scratch],
        compiler_params=_params(("arbitrary", "arbitrary"), blocks, scratch),
        name="pool_layer",
    )(x, x, first_rows, g_mix, pool_w, pool_scale, g_ffn)


def _gateup_kernel(x_ref, wg_ref, wu_ref, o_ref, wgb_ref, wub_ref):
    @pl.when(pl.program_id(1) == 0)
    def _():
        wgb_ref[...] = wg_ref[...].astype(BF16)
        wub_ref[...] = wu_ref[...].astype(BF16)

    x = x_ref[...]
    g = jnp.dot(x, wgb_ref[...], preferred_element_type=F32)
    u = jnp.dot(x, wub_ref[...], preferred_element_type=F32)
    o_ref[...] = (g * jax.nn.sigmoid(g) * u).astype(o_ref.dtype)


def _gateup(x, wg, wu):
    m, k = x.shape
    n = wg.shape[1]
    bm = _pick(m, 1024, BF16_SUBLANES)
    bn = _pick(n, 256, V7X_LANES)
    blocks = [((bm, k), BF16), ((k, bn), F32), ((k, bn), F32), ((bm, bn), BF16)]
    scratch = [((k, bn), BF16), ((k, bn), BF16)]
    return pl.pallas_call(
        _gateup_kernel,
        grid=(n // bn, m // bm),
        in_specs=[
            pl.BlockSpec((bm, k), lambda j, i: (i, 0)),
            pl.BlockSpec((k, bn), lambda j, i: (0, j)),
            pl.BlockSpec((k, bn), lambda j, i: (0, j)),
        ],
        out_specs=pl.BlockSpec((bm, bn), lambda j, i: (i, j)),
        out_shape=jax.ShapeDtypeStruct((m, n), BF16),
        scratch_shapes=[pltpu.VMEM(s, t) for s, t in scratch],
        compiler_params=_params(("arbitrary", "arbitrary"), blocks, scratch),
        name="ffn_gateup",
    )(x, wg, wu)


def _matmul_residual_kernel(a_ref, w_ref, r_ref, o_ref):
    kk = pl.program_id(2)
    part = jnp.dot(a_ref[...], w_ref[...], preferred_element_type=F32)

    @pl.when(kk == 0)
    def _():
        o_ref[...] = r_ref[...] + part

    @pl.when(kk != 0)
    def _():
        o_ref[...] += part


def _matmul_residual(a, w, r, bk_target):
    m, k = a.shape
    n = w.shape[1]
    bm = _pick(m, 1024, BF16_SUBLANES)
    bn = _pick(n, 1024, V7X_LANES)
    bk = _pick(k, bk_target, V7X_LANES)
    blocks = [((bm, bk), BF16), ((bk, bn), BF16), ((bm, bn), F32), ((bm, bn), F32)]
    return pl.pallas_call(
        _matmul_residual_kernel,
        grid=(m // bm, n // bn, k // bk),
        in_specs=[
            pl.BlockSpec((bm, bk), lambda i, j, kk: (i, kk)),
            pl.BlockSpec((bk, bn), lambda i, j, kk: (kk, j)),
            pl.BlockSpec((bm, bn), lambda i, j, kk: (i, j)),
        ],
        out_specs=pl.BlockSpec((bm, bn), lambda i, j, kk: (i, j)),
        out_shape=jax.ShapeDtypeStruct((m, n), F32),
        compiler_params=_params(("arbitrary", "arbitrary", "arbitrary"), blocks),
        name="matmul_residual",
    )(a, w, r)


def _qkv_kernel(x_ref, g_ref, w_ref, qn_ref, kn_ref, o_ref, xn_ref, *, n_q_tiles, hd, q_scale):
    j = pl.program_id(1)

    @pl.when(j == 0)
    def _():
        xn_ref[...] = (_rms(x_ref[...], RMS_EPS) * g_ref[...]).astype(BF16)

    acc = jnp.dot(xn_ref[...], w_ref[...], preferred_element_type=F32)

    @pl.when(j < 2 * n_q_tiles)
    def _():
        gain = jnp.where(j < n_q_tiles, qn_ref[...] * q_scale, kn_ref[...])
        for c in range(acc.shape[1] // hd):
            cs = slice(c * hd, (c + 1) * hd)
            o_ref[:, cs] = (_rms(acc[:, cs], RMS_EPS) * gain).astype(o_ref.dtype)

    @pl.when(j >= 2 * n_q_tiles)
    def _():
        o_ref[...] = acc.astype(o_ref.dtype)


def _qkv_proj(h, g, w, q_norm, k_norm):
    m, d = h.shape
    n = w.shape[1]
    bm = _pick(m, 512, BF16_SUBLANES)
    bn = _pick(d, 1024, HEAD_DIM)
    kern = functools.partial(_qkv_kernel, n_q_tiles=d // bn, hd=HEAD_DIM, q_scale=HEAD_DIM ** -0.5)
    blocks = [((bm, d), F32), ((d, bn), BF16), ((bm, bn), BF16)]
    scratch = [((bm, d), BF16)]
    return pl.pallas_call(
        kern,
        grid=(m // bm, n // bn),
        in_specs=[
            pl.BlockSpec((bm, d), lambda i, j: (i, 0)),
            pl.BlockSpec((1, d), lambda i, j: (0, 0)),
            pl.BlockSpec((d, bn), lambda i, j: (0, j)),
            pl.BlockSpec((1, HEAD_DIM), lambda i, j: (0, 0)),
            pl.BlockSpec((1, HEAD_DIM), lambda i, j: (0, 0)),
        ],
        out_specs=pl.BlockSpec((bm, bn), lambda i, j: (i, j)),
        out_shape=jax.ShapeDtypeStruct((m, n), BF16),
        scratch_shapes=[pltpu.VMEM(s, t) for s, t in scratch],
        compiler_params=_params(("arbitrary", "arbitrary"), blocks, scratch),
        name="qkv_proj",
    )(h, g, w, q_norm, k_norm)


def _attn_kernel(q_ref, k_ref, v_ref, km_ref, vm_ref, lq1_ref, lk1_ref, lq2_ref, lk2_ref, sub_ref,
                 o_ref, acc_ref, m_ref, l_ref, *, tk, hd, n_meta, lambda_init):
    qi = pl.program_id(2)
    tq = q_ref.shape[0]
    nt = (((1,), (1,)), ((), ()))

    def update(mp, k, v, mask, first):
        q = q_ref[:, mp * hd:(mp + 1) * hd]
        s = lax.dot_general(q, k, nt, preferred_element_type=F32)
        if mask is not None:
            s = jnp.where(mask, s, MASK_VALUE)
        row_max = jnp.max(s, axis=1, keepdims=True)
        if first:
            m_new = row_max
        else:
            m_old = m_ref[mp][:, :1]
            m_new = jnp.maximum(m_old, row_max)
        p = jnp.exp(s - m_new)
        row_sum = jnp.sum(p, axis=1, keepdims=True)
        pv = jnp.dot(p.astype(BF16), v, preferred_element_type=F32)
        if first:
            l_new = row_sum
            acc_ref[mp] = pv
        else:
            alpha = jnp.exp(m_old - m_new)
            l_new = alpha * l_ref[mp][:, :1] + row_sum
            acc_ref[mp] = alpha * acc_ref[mp] + pv
        m_ref[mp] = jnp.broadcast_to(m_new, (tq, V7X_LANES))
        l_ref[mp] = jnp.broadcast_to(l_new, (tq, V7X_LANES))

    meta_mask = lax.broadcasted_iota(jnp.int32, (tq, km_ref.shape[0]), 1) < n_meta
    for mp in range(2):
        update(mp, km_ref[:, mp * hd:(mp + 1) * hd], vm_ref[...], meta_mask, True)

    n_full = (qi * tq) // tk

    def full_tile(kt, carry):
        k0 = pl.multiple_of(kt * tk, tk)
        for mp in range(2):
            update(mp, k_ref[pl.ds(k0, tk), mp * hd:(mp + 1) * hd], v_ref[pl.ds(k0, tk), :], None, False)
        return carry

    lax.fori_loop(0, n_full, full_tile, 0)

    row = lax.broadcasted_iota(jnp.int32, (tq, tk), 0)
    col = lax.broadcasted_iota(jnp.int32, (tq, tk), 1)
    for dt in range(tq // tk):
        k0 = pl.multiple_of(qi * tq + dt * tk, tk)
        diag_mask = col + dt * tk <= row
        for mp in range(2):
            update(mp, k_ref[pl.ds(k0, tk), mp * hd:(mp + 1) * hd], v_ref[pl.ds(k0, tk), :], diag_mask, False)

    lam = (jnp.exp(jnp.sum(lq1_ref[...] * lk1_ref[...], axis=-1, keepdims=True))
           - jnp.exp(jnp.sum(lq2_ref[...] * lk2_ref[...], axis=-1, keepdims=True)) + lambda_init)
    o1 = acc_ref[0] / l_ref[0][:, :1]
    o2 = acc_ref[1] / l_ref[1][:, :1]
    o = o1 - lam * o2
    o = _rms(o, SUBLN_EPS) * sub_ref[...] * (1.0 - lambda_init)
    o_ref[...] = o.astype(o_ref.dtype)


def _attn_bounded_kernel(shift_ref, q_ref, k_ref, v_ref, km_ref, vm_ref, lq1_ref, lk1_ref, lq2_ref, lk2_ref,
                         sub_ref, o_ref, kt_ref, acc_ref, ls_ref, *, hd, n_meta, lambda_init):
    qi = pl.program_id(2)
    tq = q_ref.shape[0]
    n_kt, _, tk = kt_ref.shape
    shift = shift_ref[0, 0]

    @pl.when(qi == 0)
    def _():
        def transpose_tile(c, carry):
            r0 = pl.multiple_of(c * tk, tk)
            kt_ref[c] = k_ref[pl.ds(r0, tk), :].T
            return carry

        lax.fori_loop(0, n_kt, transpose_tile, 0)

    def accumulate(mp, s, v, mask, first):
        p = jnp.exp(s - shift)
        if mask is not None:
            p = jnp.where(mask, p, 0.0)
        part = p[:, 0:V7X_LANES]
        for c in range(1, p.shape[1] // V7X_LANES):
            part = part + p[:, c * V7X_LANES:(c + 1) * V7X_LANES]
        pv = jnp.dot(p.astype(BF16), v, preferred_element_type=F32)
        if first:
            ls_ref[mp] = part
            acc_ref[mp] = pv
        else:
            ls_ref[mp] += part
            acc_ref[mp] += pv

    nt = (((1,), (1,)), ((), ()))
    meta_mask = lax.broadcasted_iota(jnp.int32, (tq, km_ref.shape[0]), 1) < n_meta
    for mp in range(2):
        hs = slice(mp * hd, (mp + 1) * hd)
        s = lax.dot_general(q_ref[:, hs], km_ref[:, hs], nt, preferred_element_type=F32)
        accumulate(mp, s, vm_ref[...], meta_mask, True)

    def kv_tile(kt, mask):
        k0 = pl.multiple_of(kt * tk, tk)
        v = v_ref[pl.ds(k0, tk), :]
        for mp in range(2):
            hs = slice(mp * hd, (mp + 1) * hd)
            s = jnp.dot(q_ref[:, hs], kt_ref[kt, hs, :], preferred_element_type=F32)
            accumulate(mp, s, v, mask, False)

    def two_full_tiles(pair, carry):
        kv_tile(2 * pair, None)
        kv_tile(2 * pair + 1, None)
        return carry

    lax.fori_loop(0, qi // 2, two_full_tiles, 0)

    @pl.when(qi % 2 == 1)
    def _():
        kv_tile(qi - 1, None)

    row = lax.broadcasted_iota(jnp.int32, (tq, tk), 0)
    col = lax.broadcasted_iota(jnp.int32, (tq, tk), 1)
    kv_tile(qi, col <= row)

    lam = (jnp.exp(jnp.sum(lq1_ref[...] * lk1_ref[...], axis=-1, keepdims=True))
           - jnp.exp(jnp.sum(lq2_ref[...] * lk2_ref[...], axis=-1, keepdims=True)) + lambda_init)
    o1 = acc_ref[0] / jnp.sum(ls_ref[0], axis=1, keepdims=True)
    o2 = acc_ref[1] / jnp.sum(ls_ref[1], axis=1, keepdims=True)
    o = o1 - lam * o2
    o = _rms(o, SUBLN_EPS) * sub_ref[...] * (1.0 - lambda_init)
    o_ref[...] = o.astype(o_ref.dtype)


SAFE_LOGIT_BOUND = 32.0


def _diff_attention(qkv, qkv_meta, logit_bound, lq1, lk1, lq2, lk2, subln, batch, n_meta, lambda_init):
    t, n3 = qkv.shape
    d = n3 // 3
    l = t // batch
    hw = 2 * HEAD_DIM
    heads = d // hw
    tq = _pick(l, 512, V7X_LANES)
    nq = l // tq
    mrows = qkv_meta.shape[0]
    vec = pl.BlockSpec((1, HEAD_DIM), lambda b, h, i: (0, 0))
    blocks = [((tq, hw), BF16), ((l, hw), BF16), ((l, hw), BF16), ((mrows, hw), BF16), ((mrows, hw), BF16),
              ((tq, hw), BF16)]
    in_specs = [
        pl.BlockSpec((tq, hw), lambda b, h, i: (b * nq + i, h)),
        pl.BlockSpec((l, hw), lambda b, h, i: (b, heads + h)),
        pl.BlockSpec((l, hw), lambda b, h, i: (b, 2 * heads + h)),
        pl.BlockSpec((mrows, hw), lambda b, h, i: (0, heads + h)),
        pl.BlockSpec((mrows, hw), lambda b, h, i: (0, 2 * heads + h)),
        vec, vec, vec, vec,
        pl.BlockSpec((1, hw), lambda b, h, i: (0, 0)),
    ]
    common = dict(
        grid=(batch, heads, nq),
        out_specs=pl.BlockSpec((tq, hw), lambda b, h, i: (b * nq + i, h)),
        out_shape=jax.ShapeDtypeStruct((t, d), BF16),
    )
    operands = (qkv, qkv, qkv, qkv_meta, qkv_meta, lq1, lk1, lq2, lk2, subln)

    def running_max(_):
        kern = functools.partial(_attn_kernel, tk=tq, hd=HEAD_DIM, n_meta=n_meta, lambda_init=lambda_init)
        scratch = [((2, tq, hw), F32), ((2, tq, V7X_LANES), F32), ((2, tq, V7X_LANES), F32)]
        return pl.pallas_call(
            kern, in_specs=in_specs,
            scratch_shapes=[pltpu.VMEM(s, ty) for s, ty in scratch],
            compiler_params=_params(("arbitrary", "arbitrary", "arbitrary"), blocks, scratch),
            name="diff_attention_running_max", **common,
        )(*operands)

    def bounded(shift):
        kern = functools.partial(_attn_bounded_kernel, hd=HEAD_DIM, n_meta=n_meta, lambda_init=lambda_init)
        scratch = [((l // tq, hw, tq), BF16), ((2, tq, hw), F32), ((2, tq, V7X_LANES), F32)]
        return pl.pallas_call(
            kern,
            in_specs=[pl.BlockSpec((1, 1), lambda b, h, i: (0, 0), memory_space=pltpu.SMEM)] + in_specs,
            scratch_shapes=[pltpu.VMEM(s, ty) for s, ty in scratch],
            compiler_params=_params(("arbitrary", "arbitrary", "arbitrary"), blocks, scratch),
            name="diff_attention_bounded", **common,
        )(shift, *operands)

    shift = logit_bound.reshape(1, 1).astype(F32)
    return lax.cond(logit_bound <= SAFE_LOGIT_BOUND, bounded, running_max, shift)


def _router_kernel(x_ref, g_ref, r_ref, up_ref, rt_ref, *, n_exp):
    x = x_ref[...]
    u = _rms(x, RMS_EPS) * g_ref[...]
    logits = jnp.dot(u, r_ref[...], preferred_element_type=F32, precision=lax.Precision.HIGHEST)
    lane = lax.broadcasted_iota(jnp.int32, logits.shape, 1).astype(F32)
    lg = jnp.where(lane < n_exp, logits, -jnp.inf)
    m1 = jnp.max(lg, axis=1, keepdims=True)
    i1 = jnp.min(jnp.where(lg == m1, lane, float(V7X_LANES)), axis=1, keepdims=True)
    lg2 = jnp.where(lane == i1, -jnp.inf, lg)
    m2 = jnp.max(lg2, axis=1, keepdims=True)
    i2 = jnp.min(jnp.where(lg2 == m2, lane, float(V7X_LANES)), axis=1, keepdims=True)
    e = jnp.exp(m2 - m1)
    g1 = 1.0 / (1.0 + e)
    g2 = e / (1.0 + e)
    rt = jnp.where(lane == 0, i1, jnp.where(lane == 1, i2, jnp.where(lane == 2, g1, jnp.where(lane == 3, g2, 0.0))))
    rt_ref[...] = rt
    bits = lax.bitcast_convert_type(u.astype(BF16).astype(F32), jnp.uint32)
    d2 = x.shape[1] // 2
    up_ref[...] = (bits[:, :d2] >> 16) | (bits[:, d2:] & jnp.uint32(0xFFFF0000))


def _router(h, g, router_padded, n_exp):
    m, d = h.shape
    bm = _pick(m, 256, 8)
    kern = functools.partial(_router_kernel, n_exp=n_exp)
    blocks = [((bm, d), F32), ((d, V7X_LANES), F32), ((bm, d // 2), jnp.uint32), ((bm, V7X_LANES), F32)]
    return pl.pallas_call(
        kern,
        grid=(m // bm,),
        in_specs=[
            pl.BlockSpec((bm, d), lambda i: (i, 0)),
            pl.BlockSpec((1, d), lambda i: (0, 0)),
            pl.BlockSpec((d, V7X_LANES), lambda i: (0, 0)),
        ],
        out_specs=[pl.BlockSpec((bm, d // 2), lambda i: (i, 0)), pl.BlockSpec((bm, V7X_LANES), lambda i: (i, 0))],
        out_shape=[jax.ShapeDtypeStruct((m, d // 2), jnp.uint32), jax.ShapeDtypeStruct((m, V7X_LANES), F32)],
        compiler_params=_params(("arbitrary",), blocks),
        name="router",
    )(h, g, router_padded)


def _dispatch_kernel(pos_ref, u_ref, xs_in_ref, xs_ref, sem):
    del xs_in_ref
    bm = u_ref.shape[0]

    def row_copy(r, s):
        return pltpu.make_async_copy(u_ref.at[pl.ds(r, 1)], xs_ref.at[pl.ds(pos_ref[0, 0, 2 * r + s], 1)], sem)

    def start(r, c):
        row_copy(r, 0).start()
        row_copy(r, 1).start()
        return c

    def wait(r, c):
        row_copy(r, 0).wait()
        row_copy(r, 1).wait()
        return c

    lax.fori_loop(0, bm, start, 0)
    lax.fori_loop(0, bm, wait, 0)


def _dispatch(up, pos_tiles, xs_zero):
    m, d2 = up.shape
    bm = pos_tiles.shape[2] // TOP_K
    blocks = [((bm, d2), jnp.uint32)]
    return pl.pallas_call(
        _dispatch_kernel,
        grid=(m // bm,),
        in_specs=[
            pl.BlockSpec((1, 1, TOP_K * bm), lambda i: (i, 0, 0), memory_space=pltpu.SMEM),
            pl.BlockSpec((bm, d2), lambda i: (i, 0)),
            pl.BlockSpec(memory_space=pl.ANY),
        ],
        out_specs=pl.BlockSpec(memory_space=pl.ANY),
        out_shape=jax.ShapeDtypeStruct(xs_zero.shape, xs_zero.dtype),
        scratch_shapes=[pltpu.SemaphoreType.DMA(())],
        input_output_aliases={2: 0},
        compiler_params=_params(("arbitrary",), blocks),
        name="moe_dispatch",
    )(pos_tiles, up, xs_zero)


def _combine_kernel(pos_ref, h_ref, rt_ref, y_ref, o_ref, buf_ref, sem):
    bm = h_ref.shape[0]

    def row_copy(r, s):
        return pltpu.make_async_copy(y_ref.at[pl.ds(pos_ref[0, 0, 2 * r + s], 1)], buf_ref.at[s, pl.ds(r, 1)], sem)

    def start(r, c):
        row_copy(r, 0).start()
        row_copy(r, 1).start()
        return c

    def wait(r, c):
        row_copy(r, 0).wait()
        row_copy(r, 1).wait()
        return c

    lax.fori_loop(0, bm, start, 0)
    lax.fori_loop(0, bm, wait, 0)
    rt = rt_ref[...]
    o_ref[...] = h_ref[...] + rt[:, 2:3] * buf_ref[0] + rt[:, 3:4] * buf_ref[1]


def _combine(h, route, y, pos_tiles):
    m, d = h.shape
    bm = pos_tiles.shape[2] // TOP_K
    blocks = [((bm, d), F32), ((bm, V7X_LANES), F32), ((bm, d), F32)]
    scratch = [((TOP_K, bm, d), F32)]
    return pl.pallas_call(
        _combine_kernel,
        grid=(m // bm,),
        in_specs=[
            pl.BlockSpec((1, 1, TOP_K * bm), lambda i: (i, 0, 0), memory_space=pltpu.SMEM),
            pl.BlockSpec((bm, d), lambda i: (i, 0)),
            pl.BlockSpec((bm, V7X_LANES), lambda i: (i, 0)),
            pl.BlockSpec(memory_space=pl.ANY),
        ],
        out_specs=pl.BlockSpec((bm, d), lambda i: (i, 0)),
        out_shape=jax.ShapeDtypeStruct((m, d), F32),
        scratch_shapes=[pltpu.VMEM(scratch[0][0], F32), pltpu.SemaphoreType.DMA(())],
        compiler_params=_params(("arbitrary",), blocks, scratch),
        name="moe_combine",
    )(pos_tiles, h, route, y)


def _new_expert(te_ref, i):
    return jnp.logical_or(i == 0, te_ref[i] != te_ref[jnp.maximum(i - 1, 0)])


def _expert_gateup_kernel(te_ref, nu_ref, x_ref, wg_ref, wu_ref, o_ref, wgb_ref, wub_ref):
    i = pl.program_id(1)
    d2 = x_ref.shape[1]
    used = i < nu_ref[0]

    @pl.when(jnp.logical_and(used, _new_expert(te_ref, i)))
    def _():
        wgb_ref[...] = wg_ref[0].astype(BF16)
        wub_ref[...] = wu_ref[0].astype(BF16)

    @pl.when(used)
    def _():
        w = x_ref[...]
        lo = lax.bitcast_convert_type(w << 16, F32).astype(BF16)
        hi = lax.bitcast_convert_type(w & jnp.uint32(0xFFFF0000), F32).astype(BF16)
        g = (jnp.dot(lo, wgb_ref[:d2], preferred_element_type=F32)
             + jnp.dot(hi, wgb_ref[d2:], preferred_element_type=F32))
        u = (jnp.dot(lo, wub_ref[:d2], preferred_element_type=F32)
             + jnp.dot(hi, wub_ref[d2:], preferred_element_type=F32))
        o_ref[...] = (g * jax.nn.sigmoid(g) * u).astype(o_ref.dtype)

    @pl.when(jnp.logical_not(used))
    def _():
        o_ref[...] = jnp.zeros_like(o_ref)


def _expert_gateup(xs, wg, wu, tile_expert, n_used, tm):
    p, d2 = xs.shape
    _, d, fe = wg.shape
    bn = _pick(fe, 256, V7X_LANES)
    blocks = [((tm, d2), jnp.uint32), ((1, d, bn), F32), ((1, d, bn), F32), ((tm, bn), BF16)]
    scratch = [((d, bn), BF16), ((d, bn), BF16)]
    grid_spec = pltpu.PrefetchScalarGridSpec(
        num_scalar_prefetch=2,
        grid=(fe // bn, p // tm),
        in_specs=[
            pl.BlockSpec((tm, d2), lambda j, i, te, nu: (jnp.minimum(i, nu[0] - 1), 0)),
            pl.BlockSpec((1, d, bn), lambda j, i, te, nu: (te[i], 0, j)),
            pl.BlockSpec((1, d, bn), lambda j, i, te, nu: (te[i], 0, j)),
        ],
        out_specs=pl.BlockSpec((tm, bn), lambda j, i, te, nu: (i, j)),
        scratch_shapes=[pltpu.VMEM(s, t) for s, t in scratch],
    )
    return pl.pallas_call(
        _expert_gateup_kernel,
        grid_spec=grid_spec,
        out_shape=jax.ShapeDtypeStruct((p, fe), BF16),
        compiler_params=_params(("arbitrary", "arbitrary"), blocks, scratch),
        name="expert_gateup",
    )(tile_expert, n_used, xs, wg, wu)


def _expert_down_kernel(te_ref, nu_ref, a_ref, w_ref, o_ref, wb_ref):
    i = pl.program_id(1)
    used = i < nu_ref[0]

    @pl.when(jnp.logical_and(used, _new_expert(te_ref, i)))
    def _():
        wb_ref[...] = w_ref[0].astype(BF16)

    @pl.when(used)
    def _():
        o_ref[...] = jnp.dot(a_ref[...], wb_ref[...], preferred_element_type=F32)

    @pl.when(jnp.logical_not(used))
    def _():
        o_ref[...] = jnp.zeros_like(o_ref)


def _expert_down(a, wd, tile_expert, n_used, tm):
    p, fe = a.shape
    d = wd.shape[2]
    bn = _pick(d, 512, V7X_LANES)
    blocks = [((tm, fe), BF16), ((1, fe, bn), F32), ((tm, bn), F32)]
    scratch = [((fe, bn), BF16)]
    grid_spec = pltpu.PrefetchScalarGridSpec(
        num_scalar_prefetch=2,
        grid=(d // bn, p // tm),
        in_specs=[
            pl.BlockSpec((tm, fe), lambda j, i, te, nu: (jnp.minimum(i, nu[0] - 1), 0)),
            pl.BlockSpec((1, fe, bn), lambda j, i, te, nu: (te[i], 0, j)),
        ],
        out_specs=pl.BlockSpec((tm, bn), lambda j, i, te, nu: (i, j)),
        scratch_shapes=[pltpu.VMEM(s, t) for s, t in scratch],
    )
    return pl.pallas_call(
        _expert_down_kernel,
        grid_spec=grid_spec,
        out_shape=jax.ShapeDtypeStruct((p, d), F32),
        compiler_params=_params(("arbitrary", "arbitrary"), blocks, scratch),
        name="expert_down",
    )(tile_expert, n_used, a, wd)


def _routing_tables(route, n_exp, tm, n_tiles):
    experts = route[:, :TOP_K].astype(jnp.int32).reshape(-1)
    onehot = (experts[:, None] == jnp.arange(n_exp, dtype=jnp.int32)[None, :]).astype(jnp.int32)
    csum = jnp.cumsum(onehot, axis=0)
    rank = jnp.sum(csum * onehot, axis=1) - 1
    counts = csum[-1]
    padded = ((counts + tm - 1) // tm) * tm
    ends = jnp.cumsum(padded)
    starts = ends - padded
    pos = jnp.sum(starts[None, :] * onehot, axis=1) + rank
    n_used = (ends[-1] // tm).astype(jnp.int32)
    tile_row = jnp.minimum(jnp.arange(n_tiles, dtype=jnp.int32), n_used - 1) * tm
    tile_expert = jnp.sum((tile_row[:, None] >= ends[None, :]).astype(jnp.int32), axis=1)
    return pos, jnp.minimum(tile_expert, n_exp - 1).astype(jnp.int32), n_used.reshape(1)


def kernel(x, meta_tokens, norm_mix, norm_ffn, pool_w, pool_scale, ffn_w_gate, ffn_w_up, ffn_w_down,
           w_qkv, q_norm, k_norm, lambda_q1, lambda_k1, lambda_q2, lambda_k2, subln, w_o,
           router, exp_w_gate, exp_w_up, exp_w_down):
    batch, seq, d = x.shape
    n_meta = meta_tokens.shape[0]
    n_exp = router.shape[-1]
    assert norm_mix.shape[0] == 2, "one pooling layer followed by one attention layer"
    halo = max(POOL_WINDOWS)
    assert n_meta >= halo and n_meta <= V7X_LANES
    t = batch * seq
    lambda_init = 0.8 - 0.6 * math.exp(-0.3 * 1)

    row = lambda v: v.reshape(1, -1).astype(F32)
    pool_wb = pool_w[0].astype(BF16)
    wg, wu = (w.reshape(w.shape[1:]).astype(F32) for w in (ffn_w_gate, ffn_w_up))
    wd = ffn_w_down[0].astype(BF16)
    wqkv, wo = w_qkv[0].astype(BF16), w_o[0].astype(BF16)
    ewg, ewu, ewd = (w.reshape(w.shape[1:]).astype(F32) for w in (exp_w_gate, exp_w_up, exp_w_down))
    router_padded = jnp.pad(router[0].astype(F32), ((0, 0), (0, V7X_LANES - n_exp)))

    def layer0_and_qkv(tokens, first_rows, pos0):
        b, l, _ = tokens.shape
        h1, u2 = _pool_layer(tokens, first_rows, row(norm_mix[0]), pool_wb, row(pool_scale[0]),
                             row(norm_ffn[0]), pos0)
        a = _gateup(u2.reshape(b * l, d), wg, wu)
        h2 = _matmul_residual(a, wd, h1.reshape(b * l, d), 2048)
        qkv = _qkv_proj(h2, row(norm_mix[1]), wqkv, row(q_norm[0]), row(k_norm[0]))
        return h2, qkv

    meta = meta_tokens.astype(F32)
    _, qkv_meta = layer0_and_qkv(meta[None], jnp.zeros((halo, d), F32), 0)
    qkv_meta = jnp.pad(qkv_meta, ((0, V7X_LANES - n_meta), (0, 0)))
    h2, qkv = layer0_and_qkv(x, meta[n_meta - halo:], n_meta)

    logit_bound = 1.01 * math.sqrt(HEAD_DIM) * jnp.max(jnp.abs(q_norm[0])) * jnp.max(jnp.abs(k_norm[0]))
    o = _diff_attention(qkv, qkv_meta, logit_bound.astype(F32), row(lambda_q1[0]), row(lambda_k1[0]),
                        row(lambda_q2[0]), row(lambda_k2[0]), row(subln[0]), batch, n_meta, lambda_init)
    h3 = _matmul_residual(o, wo, h2, d)

    up, route = _router(h3, row(norm_ffn[1]), router_padded, n_exp)
    tm = _pick(t, 512, BF16_SUBLANES)
    n_tiles = (TOP_K * t) // tm + n_exp
    pos, tile_expert, n_used = _routing_tables(route, n_exp, tm, n_tiles)
    bm_rows = _pick(t, 128, 8)
    pos_tiles = pos.reshape(t // bm_rows, 1, TOP_K * bm_rows)
    xs = _dispatch(up, pos_tiles, jnp.zeros((n_tiles * tm, d // 2), jnp.uint32))
    a = _expert_gateup(xs, ewg, ewu, tile_expert, n_used, tm)
    y = _expert_down(a, ewd, tile_expert, n_used, tm)
    out = _combine(h3, route, y, pos_tiles)
    return out.reshape(batch, seq, d)
```

```python
import functools
import math

import jax
import jax.numpy as jnp
from jax import lax
from jax.experimental import pallas as pl
from jax.experimental.pallas import tpu as pltpu

POOL_WINDOWS = (2, 4, 8, 16)
HEAD_DIM = 128
TOP_K = 2
RMS_EPS = 1e-6
SUBLN_EPS = 1e-5
MASK_VALUE = -1e30

V7X_LANES = 128
V7X_VMEM_BYTES = 64 * 1024 * 1024
V7X_VMEM_RESERVE_BYTES = 6 * 1024 * 1024
BF16_SUBLANES = 16

F32 = jnp.float32
BF16 = jnp.bfloat16


def _pick(dim, target, mult):
    for b in range(min(dim, target), 0, -1):
        if dim % b == 0 and b % mult == 0:
            return b
    return dim


def _nbytes(shape, dtype):
    return math.prod(shape) * jnp.dtype(dtype).itemsize


def _params(semantics, pipelined, scratch=()):
    est = 2 * sum(_nbytes(s, d) for s, d in pipelined) + sum(_nbytes(s, d) for s, d in scratch)
    limit = min(V7X_VMEM_BYTES - V7X_VMEM_RESERVE_BYTES, est + 16 * 1024 * 1024)
    return pltpu.CompilerParams(dimension_semantics=semantics, vmem_limit_bytes=limit)


def _rms(v, eps):
    return v * lax.rsqrt(jnp.mean(v * v, axis=-1, keepdims=True) + eps)


def _pool_kernel(x_ref, prev_ref, first_ref, gm_ref, pw_ref, ps_ref, gf_ref, h_ref, u_ref, ext_ref,
                 *, windows, halo, pos0):
    i = pl.program_id(1)
    tm = x_ref.shape[1]
    c = pw_ref.shape[1]
    x = x_ref[0]
    gm = gm_ref[...]
    before = jnp.where(i == 0, first_ref[...], prev_ref[0])
    ext_ref[0:halo] = _rms(before, RMS_EPS) * gm
    ext_ref[halo:] = _rms(x, RMS_EPS) * gm
    pos = pos0 + i * tm + lax.broadcasted_iota(jnp.int32, (tm, 1), 0)
    for g, w in enumerate(windows):
        cs = slice(g * c, (g + 1) * c)
        u = ext_ref[halo:halo + tm, cs]
        acc = u
        for s in range(1, w):
            acc = acc + ext_ref[halo - s:halo - s + tm, cs]
        inv_cnt = 1.0 / jnp.minimum(pos + 1, w).astype(F32)
        pooled = acc * inv_cnt - u
        out = jnp.dot(pooled.astype(BF16), pw_ref[g], preferred_element_type=F32)
        h_ref[0, :, cs] = x[:, cs] + out * ps_ref[:, cs]
    h = h_ref[0]
    u_ref[0] = (_rms(h, RMS_EPS) * gf_ref[...]).astype(BF16)


def _pool_layer(x, first_rows, g_mix, pool_w, pool_scale, g_ffn, pos0):
    b, l, d = x.shape
    halo = max(POOL_WINDOWS)
    assert first_rows.shape == (halo, d)
    tm = _pick(l, 256, halo)
    nb = tm // halo
    g, c, _ = pool_w.shape
    kern = functools.partial(_pool_kernel, windows=POOL_WINDOWS, halo=halo, pos0=pos0)
    row = lambda bi, i: (bi, i, 0)
    const2 = lambda bi, i: (0, 0)
    blocks = [((1, tm, d), F32), ((1, halo, d), F32), ((halo, d), F32), ((g, c, c), BF16),
              ((1, tm, d), F32), ((1, tm, d), BF16)]
    scratch = [((halo + tm, d), F32)]
    return pl.pallas_call(
        kern,
        grid=(b, l // tm),
        in_specs=[
            pl.BlockSpec((1, tm, d), row),
            pl.BlockSpec((1, halo, d), lambda bi, i: (bi, jnp.maximum(i * nb - 1, 0), 0)),
            pl.BlockSpec((halo, d), const2),
            pl.BlockSpec((1, d), const2),
            pl.BlockSpec((g, c, c), lambda bi, i: (0, 0, 0)),
            pl.BlockSpec((1, d), const2),
            pl.BlockSpec((1, d), const2),
        ],
        out_specs=[pl.BlockSpec((1, tm, d), row), pl.BlockSpec((1, tm, d), row)],
        out_shape=[jax.ShapeDtypeStruct((b, l, d), F32), jax.ShapeDtypeStruct((b, l, d), BF16)],
        scratch_shapes=[pltpu.VMEM(s, t) for s, t in scratch],
        compiler_params=_params(("arbitrary", "arbitrary"), blocks, scratch),
        name="pool_layer",
    )(x, x, first_rows, g_mix, pool_w, pool_scale, g_ffn)


def _runs_from_tile_expert(tile_expert, n_used, n_exp):
    te = tile_expert
    prev = jnp.concatenate([te[:1] - 1, te[:-1]])
    start = (te != prev).astype(jnp.int32)
    ids = jnp.arange(n_exp, dtype=jnp.int32)
    present = jnp.any(te[None, :] == ids[:, None], axis=1)
    later = jnp.logical_and(present[None, :], ids[None, :] > ids[:, None])
    next_of = jnp.min(jnp.where(later, ids[None, :], n_exp), axis=1)
    next_of = jnp.where(next_of == n_exp, te[0], next_of).astype(jnp.int32)
    return te, n_used, start, next_of[te], (te == te[-1]).astype(jnp.int32)


def _staged_kernel(te_ref, nu_ref, st_ref, nx_ref, lr_ref, x_ref, *rest, n_w, packed):
    w_hbm = rest[:n_w]
    o_ref, stage_ref, wb_ref, sem = rest[n_w:]
    j = pl.program_id(0)
    i = pl.program_id(1)
    bn = o_ref.shape[1]
    used = i < nu_ref[0]

    def copies(e, jj):
        c0 = pl.multiple_of(jj * bn, bn)
        return [pltpu.make_async_copy(w.at[e, :, pl.ds(c0, bn)], stage_ref.at[n], sem.at[n])
                for n, w in enumerate(w_hbm)]

    @pl.when(jnp.logical_and(used, st_ref[i] == 1))
    def _():
        @pl.when(jnp.logical_and(j == 0, i == 0))
        def _():
            for c in copies(te_ref[0], 0):
                c.start()

        for c in copies(te_ref[i], j):
            c.wait()
        wb_ref[...] = stage_ref[...].astype(BF16)
        j_next = j + lr_ref[i]

        @pl.when(j_next < pl.num_programs(0))
        def _():
            for c in copies(nx_ref[i], j_next):
                c.start()

    @pl.when(used)
    def _():
        if packed:
            d2 = x_ref.shape[1]
            w = x_ref[...]
            parts = [(lax.bitcast_convert_type(w << 16, F32).astype(BF16), slice(0, d2)),
                     (lax.bitcast_convert_type(w & jnp.uint32(0xFFFF0000), F32).astype(BF16), slice(d2, 2 * d2))]
        else:
            parts = [(x_ref[...], slice(None))]
        prods = [sum(jnp.dot(xp, wb_ref[n, ks], preferred_element_type=F32) for xp, ks in parts)
                 for n in range(n_w)]
        if n_w == 2:
            g, u = prods
            o_ref[...] = (g * jax.nn.sigmoid(g) * u).astype(o_ref.dtype)
        else:
            o_ref[...] = prods[0].astype(o_ref.dtype)

    @pl.when(jnp.logical_not(used))
    def _():
        o_ref[...] = jnp.zeros_like(o_ref)


def _staged_matmul(x, weights, runs, tm, bn_target, out_dtype, packed, name):
    rows = x.shape[0]
    _, k, n = weights[0].shape
    n_w = len(weights)
    bn = _pick(n, bn_target, V7X_LANES)
    kern = functools.partial(_staged_kernel, n_w=n_w, packed=packed)
    blocks = [((tm, x.shape[1]), x.dtype), ((tm, bn), out_dtype)]
    scratch = [((n_w, k, bn), F32), ((n_w, k, bn), BF16)]
    grid_spec = pltpu.PrefetchScalarGridSpec(
        num_scalar_prefetch=5,
        grid=(n // bn, rows // tm),
        in_specs=[pl.BlockSpec((tm, x.shape[1]), lambda j, i, te, nu, *_: (jnp.minimum(i, nu[0] - 1), 0))]
        + [pl.BlockSpec(memory_space=pl.ANY)] * n_w,
        out_specs=pl.BlockSpec((tm, bn), lambda j, i, *_: (i, j)),
        scratch_shapes=[pltpu.VMEM(s, t) for s, t in scratch] + [pltpu.SemaphoreType.DMA((n_w,))],
    )
    return pl.pallas_call(
        kern,
        grid_spec=grid_spec,
        out_shape=jax.ShapeDtypeStruct((rows, n), out_dtype),
        compiler_params=_params(("arbitrary", "arbitrary"), blocks, scratch),
        name=name,
    )(*runs, x, *weights)


def _gateup(x, wg, wu):
    m = x.shape[0]
    bm = _pick(m, 1024, BF16_SUBLANES)
    n_tiles = m // bm
    runs = _runs_from_tile_expert(jnp.zeros((n_tiles,), jnp.int32), jnp.full((1,), n_tiles, jnp.int32), 1)
    return _staged_matmul(x, (wg[None], wu[None]), runs, bm, 512, BF16, False, "ffn_gateup")


def _matmul_residual_kernel(a_ref, w_ref, r_ref, o_ref):
    kk = pl.program_id(2)
    part = jnp.dot(a_ref[...], w_ref[...], preferred_element_type=F32)

    @pl.when(kk == 0)
    def _():
        o_ref[...] = r_ref[...] + part

    @pl.when(kk != 0)
    def _():
        o_ref[...] += part


def _matmul_residual(a, w, r, bk_target):
    m, k = a.shape
    n = w.shape[1]
    bm = _pick(m, 1024, BF16_SUBLANES)
    bn = _pick(n, 1024, V7X_LANES)
    bk = _pick(k, bk_target, V7X_LANES)
    blocks = [((bm, bk), BF16), ((bk, bn), BF16), ((bm, bn), F32), ((bm, bn), F32)]
    return pl.pallas_call(
        _matmul_residual_kernel,
        grid=(m // bm, n // bn, k // bk),
        in_specs=[
            pl.BlockSpec((bm, bk), lambda i, j, kk: (i, kk)),
            pl.BlockSpec((bk, bn), lambda i, j, kk: (kk, j)),
            pl.BlockSpec((bm, bn), lambda i, j, kk: (i, j)),
        ],
        out_specs=pl.BlockSpec((bm, bn), lambda i, j, kk: (i, j)),
        out_shape=jax.ShapeDtypeStruct((m, n), F32),
        compiler_params=_params(("arbitrary", "arbitrary", "arbitrary"), blocks),
        name="matmul_residual",
    )(a, w, r)


def _qkv_kernel(x_ref, g_ref, w_ref, qn_ref, kn_ref, o_ref, xn_ref, *, n_q_tiles, hd, q_scale):
    j = pl.program_id(1)

    @pl.when(j == 0)
    def _():
        xn_ref[...] = (_rms(x_ref[...], RMS_EPS) * g_ref[...]).astype(BF16)

    acc = jnp.dot(xn_ref[...], w_ref[...], preferred_element_type=F32)

    @pl.when(j < 2 * n_q_tiles)
    def _():
        gain = jnp.where(j < n_q_tiles, qn_ref[...] * q_scale, kn_ref[...])
        for c in range(acc.shape[1] // hd):
            cs = slice(c * hd, (c + 1) * hd)
            o_ref[:, cs] = (_rms(acc[:, cs], RMS_EPS) * gain).astype(o_ref.dtype)

    @pl.when(j >= 2 * n_q_tiles)
    def _():
        o_ref[...] = acc.astype(o_ref.dtype)


def _qkv_proj(h, g, w, q_norm, k_norm):
    m, d = h.shape
    n = w.shape[1]
    bm = _pick(m, 512, BF16_SUBLANES)
    bn = _pick(d, 1024, HEAD_DIM)
    kern = functools.partial(_qkv_kernel, n_q_tiles=d // bn, hd=HEAD_DIM, q_scale=HEAD_DIM ** -0.5)
    blocks = [((bm, d), F32), ((d, bn), BF16), ((bm, bn), BF16)]
    scratch = [((bm, d), BF16)]
    return pl.pallas_call(
        kern,
        grid=(m // bm, n // bn),
        in_specs=[
            pl.BlockSpec((bm, d), lambda i, j: (i, 0)),
            pl.BlockSpec((1, d), lambda i, j: (0, 0)),
            pl.BlockSpec((d, bn), lambda i, j: (0, j)),
            pl.BlockSpec((1, HEAD_DIM), lambda i, j: (0, 0)),
            pl.BlockSpec((1, HEAD_DIM), lambda i, j: (0, 0)),
        ],
        out_specs=pl.BlockSpec((bm, bn), lambda i, j: (i, j)),
        out_shape=jax.ShapeDtypeStruct((m, n), BF16),
        scratch_shapes=[pltpu.VMEM(s, t) for s, t in scratch],
        compiler_params=_params(("arbitrary", "arbitrary"), blocks, scratch),
        name="qkv_proj",
    )(h, g, w, q_norm, k_norm)


def _attn_kernel(q_ref, k_ref, v_ref, km_ref, vm_ref, lq1_ref, lk1_ref, lq2_ref, lk2_ref, sub_ref,
                 o_ref, acc_ref, m_ref, l_ref, *, tk, hd, n_meta, lambda_init):
    qi = pl.program_id(2)
    tq = q_ref.shape[0]
    nt = (((1,), (1,)), ((), ()))

    def update(mp, k, v, mask, first):
        q = q_ref[:, mp * hd:(mp + 1) * hd]
        s = lax.dot_general(q, k, nt, preferred_element_type=F32)
        if mask is not None:
            s = jnp.where(mask, s, MASK_VALUE)
        row_max = jnp.max(s, axis=1, keepdims=True)
        if first:
            m_new = row_max
        else:
            m_old = m_ref[mp][:, :1]
            m_new = jnp.maximum(m_old, row_max)
        p = jnp.exp(s - m_new)
        row_sum = jnp.sum(p, axis=1, keepdims=True)
        pv = jnp.dot(p.astype(BF16), v, preferred_element_type=F32)
        if first:
            l_new = row_sum
            acc_ref[mp] = pv
        else:
            alpha = jnp.exp(m_old - m_new)
            l_new = alpha * l_ref[mp][:, :1] + row_sum
            acc_ref[mp] = alpha * acc_ref[mp] + pv
        m_ref[mp] = jnp.broadcast_to(m_new, (tq, V7X_LANES))
        l_ref[mp] = jnp.broadcast_to(l_new, (tq, V7X_LANES))

    meta_mask = lax.broadcasted_iota(jnp.int32, (tq, km_ref.shape[0]), 1) < n_meta
    for mp in range(2):
        update(mp, km_ref[:, mp * hd:(mp + 1) * hd], vm_ref[...], meta_mask, True)

    n_full = (qi * tq) // tk

    def full_tile(kt, carry):
        k0 = pl.multiple_of(kt * tk, tk)
        for mp in range(2):
            update(mp, k_ref[pl.ds(k0, tk), mp * hd:(mp + 1) * hd], v_ref[pl.ds(k0, tk), :], None, False)
        return carry

    lax.fori_loop(0, n_full, full_tile, 0)

    row = lax.broadcasted_iota(jnp.int32, (tq, tk), 0)
    col = lax.broadcasted_iota(jnp.int32, (tq, tk), 1)
    for dt in range(tq // tk):
        k0 = pl.multiple_of(qi * tq + dt * tk, tk)
        diag_mask = col + dt * tk <= row
        for mp in range(2):
            update(mp, k_ref[pl.ds(k0, tk), mp * hd:(mp + 1) * hd], v_ref[pl.ds(k0, tk), :], diag_mask, False)

    lam = (jnp.exp(jnp.sum(lq1_ref[...] * lk1_ref[...], axis=-1, keepdims=True))
           - jnp.exp(jnp.sum(lq2_ref[...] * lk2_ref[...], axis=-1, keepdims=True)) + lambda_init)
    o1 = acc_ref[0] / l_ref[0][:, :1]
    o2 = acc_ref[1] / l_ref[1][:, :1]
    o = o1 - lam * o2
    o = _rms(o, SUBLN_EPS) * sub_ref[...] * (1.0 - lambda_init)
    o_ref[...] = o.astype(o_ref.dtype)


def _attn_bounded_kernel(shift_ref, q_ref, k_ref, v_ref, km_ref, vm_ref, lq1_ref, lk1_ref, lq2_ref, lk2_ref,
                         sub_ref, o_ref, kt_ref, acc_ref, ls_ref, *, hd, n_meta, lambda_init):
    qi = pl.program_id(2)
    tq = q_ref.shape[0]
    n_kt, _, tk = kt_ref.shape
    shift = shift_ref[0, 0]

    @pl.when(qi == 0)
    def _():
        def transpose_tile(c, carry):
            r0 = pl.multiple_of(c * tk, tk)
            kt_ref[c] = k_ref[pl.ds(r0, tk), :].T
            return carry

        lax.fori_loop(0, n_kt, transpose_tile, 0)

    def accumulate(mp, s, v, mask, first):
        p = jnp.exp(s - shift)
        if mask is not None:
            p = jnp.where(mask, p, 0.0)
        part = p[:, 0:V7X_LANES]
        for c in range(1, p.shape[1] // V7X_LANES):
            part = part + p[:, c * V7X_LANES:(c + 1) * V7X_LANES]
        pv = jnp.dot(p.astype(BF16), v, preferred_element_type=F32)
        if first:
            ls_ref[mp] = part
            acc_ref[mp] = pv
        else:
            ls_ref[mp] += part
            acc_ref[mp] += pv

    nt = (((1,), (1,)), ((), ()))
    meta_mask = lax.broadcasted_iota(jnp.int32, (tq, km_ref.shape[0]), 1) < n_meta
    for mp in range(2):
        hs = slice(mp * hd, (mp + 1) * hd)
        s = lax.dot_general(q_ref[:, hs], km_ref[:, hs], nt, preferred_element_type=F32)
        accumulate(mp, s, vm_ref[...], meta_mask, True)

    def kv_tile(kt, mask):
        k0 = pl.multiple_of(kt * tk, tk)
        v = v_ref[pl.ds(k0, tk), :]
        for mp in range(2):
            hs = slice(mp * hd, (mp + 1) * hd)
            s = jnp.dot(q_ref[:, hs], kt_ref[kt, hs, :], preferred_element_type=F32)
            accumulate(mp, s, v, mask, False)

    def two_full_tiles(pair, carry):
        kv_tile(2 * pair, None)
        kv_tile(2 * pair + 1, None)
        return carry

    lax.fori_loop(0, qi // 2, two_full_tiles, 0)

    @pl.when(qi % 2 == 1)
    def _():
        kv_tile(qi - 1, None)

    row = lax.broadcasted_iota(jnp.int32, (tq, tk), 0)
    col = lax.broadcasted_iota(jnp.int32, (tq, tk), 1)
    kv_tile(qi, col <= row)

    lam = (jnp.exp(jnp.sum(lq1_ref[...] * lk1_ref[...], axis=-1, keepdims=True))
           - jnp.exp(jnp.sum(lq2_ref[...] * lk2_ref[...], axis=-1, keepdims=True)) + lambda_init)
    o1 = acc_ref[0] / jnp.sum(ls_ref[0], axis=1, keepdims=True)
    o2 = acc_ref[1] / jnp.sum(ls_ref[1], axis=1, keepdims=True)
    o = o1 - lam * o2
    o = _rms(o, SUBLN_EPS) * sub_ref[...] * (1.0 - lambda_init)
    o_ref[...] = o.astype(o_ref.dtype)


SAFE_LOGIT_BOUND = 32.0


def _diff_attention(qkv, qkv_meta, logit_bound, lq1, lk1, lq2, lk2, subln, batch, n_meta, lambda_init):
    t, n3 = qkv.shape
    d = n3 // 3
    l = t // batch
    hw = 2 * HEAD_DIM
    heads = d // hw
    tq = _pick(l, 512, V7X_LANES)
    nq = l // tq
    mrows = qkv_meta.shape[0]
    vec = pl.BlockSpec((1, HEAD_DIM), lambda b, h, i: (0, 0))
    blocks = [((tq, hw), BF16), ((l, hw), BF16), ((l, hw), BF16), ((mrows, hw), BF16), ((mrows, hw), BF16),
              ((tq, hw), BF16)]
    in_specs = [
        pl.BlockSpec((tq, hw), lambda b, h, i: (b * nq + i, h)),
        pl.BlockSpec((l, hw), lambda b, h, i: (b, heads + h)),
        pl.BlockSpec((l, hw), lambda b, h, i: (b, 2 * heads + h)),
        pl.BlockSpec((mrows, hw), lambda b, h, i: (0, heads + h)),
        pl.BlockSpec((mrows, hw), lambda b, h, i: (0, 2 * heads + h)),
        vec, vec, vec, vec,
        pl.BlockSpec((1, hw), lambda b, h, i: (0, 0)),
    ]
    common = dict(
        grid=(batch, heads, nq),
        out_specs=pl.BlockSpec((tq, hw), lambda b, h, i: (b * nq + i, h)),
        out_shape=jax.ShapeDtypeStruct((t, d), BF16),
    )
    operands = (qkv, qkv, qkv, qkv_meta, qkv_meta, lq1, lk1, lq2, lk2, subln)

    def running_max(_):
        kern = functools.partial(_attn_kernel, tk=tq, hd=HEAD_DIM, n_meta=n_meta, lambda_init=lambda_init)
        scratch = [((2, tq, hw), F32), ((2, tq, V7X_LANES), F32), ((2, tq, V7X_LANES), F32)]
        return pl.pallas_call(
            kern, in_specs=in_specs,
            scratch_shapes=[pltpu.VMEM(s, ty) for s, ty in scratch],
            compiler_params=_params(("arbitrary", "arbitrary", "arbitrary"), blocks, scratch),
            name="diff_attention_running_max", **common,
        )(*operands)

    def bounded(shift):
        kern = functools.partial(_attn_bounded_kernel, hd=HEAD_DIM, n_meta=n_meta, lambda_init=lambda_init)
        scratch = [((l // tq, hw, tq), BF16), ((2, tq, hw), F32), ((2, tq, V7X_LANES), F32)]
        return pl.pallas_call(
            kern,
            in_specs=[pl.BlockSpec((1, 1), lambda b, h, i: (0, 0), memory_space=pltpu.SMEM)] + in_specs,
            scratch_shapes=[pltpu.VMEM(s, ty) for s, ty in scratch],
            compiler_params=_params(("arbitrary", "arbitrary", "arbitrary"), blocks, scratch),
            name="diff_attention_bounded", **common,
        )(shift, *operands)

    shift = logit_bound.reshape(1, 1).astype(F32)
    return lax.cond(logit_bound <= SAFE_LOGIT_BOUND, bounded, running_max, shift)


def _router_kernel(x_ref, g_ref, r_ref, up_ref, rt_ref, *, n_exp):
    x = x_ref[...]
    u = _rms(x, RMS_EPS) * g_ref[...]
    u_hi = u.astype(BF16)
    u_lo = (u - u_hi.astype(F32)).astype(BF16)
    both = (jnp.dot(u_hi, r_ref[...], preferred_element_type=F32)
            + jnp.dot(u_lo, r_ref[...], preferred_element_type=F32))
    logits = both + pltpu.roll(both, V7X_LANES - n_exp, 1)
    lane = lax.broadcasted_iota(jnp.int32, logits.shape, 1).astype(F32)
    lg = jnp.where(lane < n_exp, logits, -jnp.inf)
    m1 = jnp.max(lg, axis=1, keepdims=True)
    i1 = jnp.min(jnp.where(lg == m1, lane, float(V7X_LANES)), axis=1, keepdims=True)
    lg2 = jnp.where(lane == i1, -jnp.inf, lg)
    m2 = jnp.max(lg2, axis=1, keepdims=True)
    i2 = jnp.min(jnp.where(lg2 == m2, lane, float(V7X_LANES)), axis=1, keepdims=True)
    e = jnp.exp(m2 - m1)
    g1 = 1.0 / (1.0 + e)
    g2 = e / (1.0 + e)
    rt = jnp.where(lane == 0, i1, jnp.where(lane == 1, i2, jnp.where(lane == 2, g1, jnp.where(lane == 3, g2, 0.0))))
    rt_ref[...] = rt
    bits = lax.bitcast_convert_type(u_hi.astype(F32), jnp.uint32)
    d2 = x.shape[1] // 2
    up_ref[...] = (bits[:, :d2] >> 16) | (bits[:, d2:] & jnp.uint32(0xFFFF0000))


def _router(h, g, router_split, n_exp):
    m, d = h.shape
    bm = _pick(m, 256, 8)
    kern = functools.partial(_router_kernel, n_exp=n_exp)
    blocks = [((bm, d), F32), ((d, V7X_LANES), BF16), ((bm, d // 2), jnp.uint32), ((bm, V7X_LANES), F32)]
    return pl.pallas_call(
        kern,
        grid=(m // bm,),
        in_specs=[
            pl.BlockSpec((bm, d), lambda i: (i, 0)),
            pl.BlockSpec((1, d), lambda i: (0, 0)),
            pl.BlockSpec((d, V7X_LANES), lambda i: (0, 0)),
        ],
        out_specs=[pl.BlockSpec((bm, d // 2), lambda i: (i, 0)), pl.BlockSpec((bm, V7X_LANES), lambda i: (i, 0))],
        out_shape=[jax.ShapeDtypeStruct((m, d // 2), jnp.uint32), jax.ShapeDtypeStruct((m, V7X_LANES), F32)],
        compiler_params=_params(("arbitrary",), blocks),
        name="router",
    )(h, g, router_split)


def _dispatch_kernel(pos_ref, u_ref, xs_in_ref, xs_ref, sem):
    del xs_in_ref
    bm = u_ref.shape[0]

    def row_copy(r, s):
        return pltpu.make_async_copy(u_ref.at[pl.ds(r, 1)], xs_ref.at[pl.ds(pos_ref[0, 0, 2 * r + s], 1)], sem)

    def start(r, c):
        row_copy(r, 0).start(priority=0)
        row_copy(r, 1).start(priority=1)
        return c

    def wait(r, c):
        row_copy(r, 0).wait()
        row_copy(r, 1).wait()
        return c

    lax.fori_loop(0, bm, start, 0)
    lax.fori_loop(0, bm, wait, 0)


def _dispatch(up, pos_tiles, xs_zero):
    m, d2 = up.shape
    bm = pos_tiles.shape[2] // TOP_K
    blocks = [((bm, d2), jnp.uint32)]
    return pl.pallas_call(
        _dispatch_kernel,
        grid=(m // bm,),
        in_specs=[
            pl.BlockSpec((1, 1, TOP_K * bm), lambda i: (i, 0, 0), memory_space=pltpu.SMEM),
            pl.BlockSpec((bm, d2), lambda i: (i, 0)),
            pl.BlockSpec(memory_space=pl.ANY),
        ],
        out_specs=pl.BlockSpec(memory_space=pl.ANY),
        out_shape=jax.ShapeDtypeStruct(xs_zero.shape, xs_zero.dtype),
        scratch_shapes=[pltpu.SemaphoreType.DMA(())],
        input_output_aliases={2: 0},
        compiler_params=_params(("arbitrary",), blocks),
        name="moe_dispatch",
    )(pos_tiles, up, xs_zero)


def _combine_kernel(pos_ref, pos_next_ref, h_ref, rt_ref, y_ref, o_ref, buf_ref, sem):
    i = pl.program_id(0)
    bm = h_ref.shape[0]
    slot = i % 2

    def row_copy(p_ref, sl, r, s):
        return pltpu.make_async_copy(y_ref.at[pl.ds(p_ref[0, 0, 2 * r + s], 1)],
                                     buf_ref.at[sl, s, pl.ds(r, 1)], sem.at[sl])

    def issue(p_ref, sl):
        def start(r, c):
            row_copy(p_ref, sl, r, 0).start(priority=0)
            row_copy(p_ref, sl, r, 1).start(priority=1)
            return c

        lax.fori_loop(0, bm, start, 0)

    @pl.when(i == 0)
    def _():
        issue(pos_ref, 0)

    @pl.when(i + 1 < pl.num_programs(0))
    def _():
        issue(pos_next_ref, 1 - slot)

    def wait(r, c):
        row_copy(pos_ref, slot, r, 0).wait()
        row_copy(pos_ref, slot, r, 1).wait()
        return c

    lax.fori_loop(0, bm, wait, 0)
    rt = rt_ref[...]
    o_ref[...] = h_ref[...] + rt[:, 2:3] * buf_ref[slot, 0] + rt[:, 3:4] * buf_ref[slot, 1]


def _combine(h, route, y, pos_tiles):
    m, d = h.shape
    n_tiles = pos_tiles.shape[0]
    bm = pos_tiles.shape[2] // TOP_K
    blocks = [((bm, d), F32), ((bm, V7X_LANES), F32), ((bm, d), F32)]
    scratch = [((2, TOP_K, bm, d), F32)]
    pos_spec = lambda index: pl.BlockSpec((1, 1, TOP_K * bm), index, memory_space=pltpu.SMEM)
    return pl.pallas_call(
        _combine_kernel,
        grid=(n_tiles,),
        in_specs=[
            pos_spec(lambda i: (i, 0, 0)),
            pos_spec(lambda i: (jnp.minimum(i + 1, n_tiles - 1), 0, 0)),
            pl.BlockSpec((bm, d), lambda i: (i, 0)),
            pl.BlockSpec((bm, V7X_LANES), lambda i: (i, 0)),
            pl.BlockSpec(memory_space=pl.ANY),
        ],
        out_specs=pl.BlockSpec((bm, d), lambda i: (i, 0)),
        out_shape=jax.ShapeDtypeStruct((m, d), F32),
        scratch_shapes=[pltpu.VMEM(scratch[0][0], F32), pltpu.SemaphoreType.DMA((2,))],
        compiler_params=_params(("arbitrary",), blocks, scratch),
        name="moe_combine",
    )(pos_tiles, pos_tiles, h, route, y)


def _routing_tables(route, n_exp, tm, n_tiles):
    experts = route[:, :TOP_K].astype(jnp.int32).reshape(-1)
    onehot = (experts[:, None] == jnp.arange(n_exp, dtype=jnp.int32)[None, :]).astype(jnp.int32)
    csum = jnp.cumsum(onehot, axis=0)
    rank = jnp.sum(csum * onehot, axis=1) - 1
    counts = csum[-1]
    padded = ((counts + tm - 1) // tm) * tm
    ends = jnp.cumsum(padded)
    starts = ends - padded
    pos = jnp.sum(starts[None, :] * onehot, axis=1) + rank
    n_used = (ends[-1] // tm).astype(jnp.int32)
    tile_row = jnp.minimum(jnp.arange(n_tiles, dtype=jnp.int32), n_used - 1) * tm
    tile_expert = jnp.sum((tile_row[:, None] >= ends[None, :]).astype(jnp.int32), axis=1)
    return pos, jnp.minimum(tile_expert, n_exp - 1).astype(jnp.int32), n_used.reshape(1)


def kernel(x, meta_tokens, norm_mix, norm_ffn, pool_w, pool_scale, ffn_w_gate, ffn_w_up, ffn_w_down,
           w_qkv, q_norm, k_norm, lambda_q1, lambda_k1, lambda_q2, lambda_k2, subln, w_o,
           router, exp_w_gate, exp_w_up, exp_w_down):
    batch, seq, d = x.shape
    n_meta = meta_tokens.shape[0]
    n_exp = router.shape[-1]
    assert norm_mix.shape[0] == 2, "one pooling layer followed by one attention layer"
    halo = max(POOL_WINDOWS)
    assert n_meta >= halo and n_meta <= V7X_LANES
    t = batch * seq
    lambda_init = 0.8 - 0.6 * math.exp(-0.3 * 1)

    row = lambda v: v.reshape(1, -1).astype(F32)
    pool_wb = pool_w[0].astype(BF16)
    wg, wu = (w.reshape(w.shape[1:]).astype(F32) for w in (ffn_w_gate, ffn_w_up))
    wd = ffn_w_down[0].astype(BF16)
    wqkv, wo = w_qkv[0].astype(BF16), w_o[0].astype(BF16)
    ewg, ewu, ewd = (w.reshape(w.shape[1:]).astype(F32) for w in (exp_w_gate, exp_w_up, exp_w_down))
    r_hi = router[0].astype(BF16)
    r_lo = (router[0].astype(F32) - r_hi.astype(F32)).astype(BF16)
    router_split = jnp.pad(jnp.concatenate([r_hi, r_lo], axis=1), ((0, 0), (0, V7X_LANES - 2 * n_exp)))

    def layer0_and_qkv(tokens, first_rows, pos0):
        b, l, _ = tokens.shape
        h1, u2 = _pool_layer(tokens, first_rows, row(norm_mix[0]), pool_wb, row(pool_scale[0]),
                             row(norm_ffn[0]), pos0)
        a = _gateup(u2.reshape(b * l, d), wg, wu)
        h2 = _matmul_residual(a, wd, h1.reshape(b * l, d), 3584)
        qkv = _qkv_proj(h2, row(norm_mix[1]), wqkv, row(q_norm[0]), row(k_norm[0]))
        return h2, qkv

    meta = meta_tokens.astype(F32)
    _, qkv_meta = layer0_and_qkv(meta[None], jnp.zeros((halo, d), F32), 0)
    qkv_meta = jnp.pad(qkv_meta, ((0, V7X_LANES - n_meta), (0, 0)))
    h2, qkv = layer0_and_qkv(x, meta[n_meta - halo:], n_meta)

    logit_bound = 1.01 * math.sqrt(HEAD_DIM) * jnp.max(jnp.abs(q_norm[0])) * jnp.max(jnp.abs(k_norm[0]))
    o = _diff_attention(qkv, qkv_meta, logit_bound.astype(F32), row(lambda_q1[0]), row(lambda_k1[0]),
                        row(lambda_q2[0]), row(lambda_k2[0]), row(subln[0]), batch, n_meta, lambda_init)
    h3 = _matmul_residual(o, wo, h2, d)

    up, route = _router(h3, row(norm_ffn[1]), router_split, n_exp)
    tm = _pick(t, 512, BF16_SUBLANES)
    n_tiles = (TOP_K * t) // tm + n_exp
    pos, tile_expert, n_used = _routing_tables(route, n_exp, tm, n_tiles)
    bm_rows = _pick(t, 128, 8)
    pos_tiles = pos.reshape(t // bm_rows, 1, TOP_K * bm_rows)
    xs = _dispatch(up, pos_tiles, jnp.zeros((n_tiles * tm, d // 2), jnp.uint32))
    runs = _runs_from_tile_expert(tile_expert, n_used, n_exp)
    a = _staged_matmul(xs, (ewg, ewu), runs, tm, 512, BF16, True, "expert_gateup")
    y = _staged_matmul(a, (ewd,), runs, tm, 1024, F32, False, "expert_down")
    out = _combine(h3, route, y, pos_tiles)
    return out.reshape(batch, seq, d)
```

```python
import functools
import math

import jax
import jax.numpy as jnp
from jax import lax
from jax.experimental import pallas as pl
from jax.experimental.pallas import tpu as pltpu

POOL_WINDOWS = (2, 4, 8, 16)
HEAD_DIM = 128
TOP_K = 2
RMS_EPS = 1e-6
SUBLN_EPS = 1e-5
MASK_VALUE = -1e30

V7X_LANES = 128
V7X_VMEM_BYTES = 64 * 1024 * 1024
V7X_VMEM_RESERVE_BYTES = 6 * 1024 * 1024
BF16_SUBLANES = 16

F32 = jnp.float32
BF16 = jnp.bfloat16


def _pick(dim, target, mult):
    for b in range(min(dim, target), 0, -1):
        if dim % b == 0 and b % mult == 0:
            return b
    return dim


def _nbytes(shape, dtype):
    return math.prod(shape) * jnp.dtype(dtype).itemsize


def _params(semantics, pipelined, scratch=()):
    est = 2 * sum(_nbytes(s, d) for s, d in pipelined) + sum(_nbytes(s, d) for s, d in scratch)
    limit = min(V7X_VMEM_BYTES - V7X_VMEM_RESERVE_BYTES, est + 16 * 1024 * 1024)
    return pltpu.CompilerParams(dimension_semantics=semantics, vmem_limit_bytes=limit)


def _rms(v, eps):
    return v * lax.rsqrt(jnp.mean(v * v, axis=-1, keepdims=True) + eps)


def _pool_kernel(x_ref, prev_ref, first_ref, gm_ref, pw_ref, ps_ref, gf_ref, h_ref, u_ref, ext_ref,
                 *, windows, halo, pos0):
    i = pl.program_id(1)
    tm = x_ref.shape[1]
    c = pw_ref.shape[1]
    x = x_ref[0]
    gm = gm_ref[...]
    before = jnp.where(i == 0, first_ref[...], prev_ref[0])
    ext_ref[0:halo] = _rms(before, RMS_EPS) * gm
    ext_ref[halo:] = _rms(x, RMS_EPS) * gm
    pos = pos0 + i * tm + lax.broadcasted_iota(jnp.int32, (tm, 1), 0)
    for g, w in enumerate(windows):
        cs = slice(g * c, (g + 1) * c)
        u = ext_ref[halo:halo + tm, cs]
        acc = u
        for s in range(1, w):
            acc = acc + ext_ref[halo - s:halo - s + tm, cs]
        inv_cnt = 1.0 / jnp.minimum(pos + 1, w).astype(F32)
        pooled = acc * inv_cnt - u
        out = jnp.dot(pooled.astype(BF16), pw_ref[g], preferred_element_type=F32)
        h_ref[0, :, cs] = x[:, cs] + out * ps_ref[:, cs]
    h = h_ref[0]
    u_ref[0] = (_rms(h, RMS_EPS) * gf_ref[...]).astype(BF16)


def _pool_layer(x, first_rows, g_mix, pool_w, pool_scale, g_ffn, pos0):
    b, l, d = x.shape
    halo = max(POOL_WINDOWS)
    assert first_rows.shape == (halo, d)
    tm = _pick(l, 256, halo)
    nb = tm // halo
    g, c, _ = pool_w.shape
    kern = functools.partial(_pool_kernel, windows=POOL_WINDOWS, halo=halo, pos0=pos0)
    row = lambda bi, i: (bi, i, 0)
    const2 = lambda bi, i: (0, 0)
    blocks = [((1, tm, d), F32), ((1, halo, d), F32), ((halo, d), F32), ((g, c, c), BF16),
              ((1, tm, d), F32), ((1, tm, d), BF16)]
    scratch = [((halo + tm, d), F32)]
    return pl.pallas_call(
        kern,
        grid=(b, l // tm),
        in_specs=[
            pl.BlockSpec((1, tm, d), row),
            pl.BlockSpec((1, halo, d), lambda bi, i: (bi, jnp.maximum(i * nb - 1, 0), 0)),
            pl.BlockSpec((halo, d), const2),
            pl.BlockSpec((1, d), const2),
            pl.BlockSpec((g, c, c), lambda bi, i: (0, 0, 0)),
            pl.BlockSpec((1, d), const2),
            pl.BlockSpec((1, d), const2),
        ],
        out_specs=[pl.BlockSpec((1, tm, d), row), pl.BlockSpec((1, tm, d), row)],
        out_shape=[jax.ShapeDtypeStruct((b, l, d), F32), jax.ShapeDtypeStruct((b, l, d), BF16)],
        scratch_shapes=[pltpu.VMEM(s, t) for s, t in scratch],
        compiler_params=_params(("arbitrary", "arbitrary"), blocks, scratch),
        name="pool_layer",
    )(x, x, first_rows, g_mix, pool_w, pool_scale, g_ffn)


def _runs_from_tile_expert(tile_expert, n_used, n_exp):
    te = tile_expert
    prev = jnp.concatenate([te[:1] - 1, te[:-1]])
    start = (te != prev).astype(jnp.int32)
    ids = jnp.arange(n_exp, dtype=jnp.int32)
    present = jnp.any(te[None, :] == ids[:, None], axis=1)
    later = jnp.logical_and(present[None, :], ids[None, :] > ids[:, None])
    next_of = jnp.min(jnp.where(later, ids[None, :], n_exp), axis=1)
    next_of = jnp.where(next_of == n_exp, te[0], next_of).astype(jnp.int32)
    return te, n_used, start, next_of[te], (te == te[-1]).astype(jnp.int32)


def _staged_kernel(te_ref, nu_ref, st_ref, nx_ref, lr_ref, x_ref, *rest, n_w, packed):
    w_hbm = rest[:n_w]
    o_ref, stage_ref, wb_ref, sem = rest[n_w:]
    j = pl.program_id(0)
    i = pl.program_id(1)
    bn = o_ref.shape[1]
    used = i < nu_ref[0]

    def copies(e, jj):
        c0 = pl.multiple_of(jj * bn, bn)
        return [pltpu.make_async_copy(w.at[e, :, pl.ds(c0, bn)], stage_ref.at[n], sem.at[n])
                for n, w in enumerate(w_hbm)]

    @pl.when(jnp.logical_and(used, st_ref[i] == 1))
    def _():
        @pl.when(jnp.logical_and(j == 0, i == 0))
        def _():
            for c in copies(te_ref[0], 0):
                c.start()

        for c in copies(te_ref[i], j):
            c.wait()
        wb_ref[...] = stage_ref[...].astype(BF16)
        j_next = j + lr_ref[i]

        @pl.when(j_next < pl.num_programs(0))
        def _():
            for c in copies(nx_ref[i], j_next):
                c.start()

    @pl.when(used)
    def _():
        if packed:
            d2 = x_ref.shape[1]
            w = x_ref[...]
            parts = [(lax.bitcast_convert_type(w << 16, F32).astype(BF16), slice(0, d2)),
                     (lax.bitcast_convert_type(w & jnp.uint32(0xFFFF0000), F32).astype(BF16), slice(d2, 2 * d2))]
        else:
            parts = [(x_ref[...], slice(None))]
        prods = [sum(jnp.dot(xp, wb_ref[n, ks], preferred_element_type=F32) for xp, ks in parts)
                 for n in range(n_w)]
        if n_w == 2:
            g, u = prods
            o_ref[...] = (g * jax.nn.sigmoid(g) * u).astype(o_ref.dtype)
        else:
            o_ref[...] = prods[0].astype(o_ref.dtype)

    @pl.when(jnp.logical_not(used))
    def _():
        o_ref[...] = jnp.zeros_like(o_ref)


def _staged_matmul(x, weights, runs, tm, bn_target, out_dtype, packed, name):
    rows = x.shape[0]
    _, k, n = weights[0].shape
    n_w = len(weights)
    bn = _pick(n, bn_target, V7X_LANES)
    kern = functools.partial(_staged_kernel, n_w=n_w, packed=packed)
    blocks = [((tm, x.shape[1]), x.dtype), ((tm, bn), out_dtype)]
    scratch = [((n_w, k, bn), F32), ((n_w, k, bn), BF16)]
    grid_spec = pltpu.PrefetchScalarGridSpec(
        num_scalar_prefetch=5,
        grid=(n // bn, rows // tm),
        in_specs=[pl.BlockSpec((tm, x.shape[1]), lambda j, i, te, nu, *_: (jnp.minimum(i, nu[0] - 1), 0))]
        + [pl.BlockSpec(memory_space=pl.ANY)] * n_w,
        out_specs=pl.BlockSpec((tm, bn), lambda j, i, *_: (i, j)),
        scratch_shapes=[pltpu.VMEM(s, t) for s, t in scratch] + [pltpu.SemaphoreType.DMA((n_w,))],
    )
    return pl.pallas_call(
        kern,
        grid_spec=grid_spec,
        out_shape=jax.ShapeDtypeStruct((rows, n), out_dtype),
        compiler_params=_params(("arbitrary", "arbitrary"), blocks, scratch),
        name=name,
    )(*runs, x, *weights)


def _gateup(x, wg, wu):
    m = x.shape[0]
    bm = _pick(m, 1024, BF16_SUBLANES)
    n_tiles = m // bm
    runs = _runs_from_tile_expert(jnp.zeros((n_tiles,), jnp.int32), jnp.full((1,), n_tiles, jnp.int32), 1)
    return _staged_matmul(x, (wg[None], wu[None]), runs, bm, 512, BF16, False, "ffn_gateup")


def _matmul_residual_kernel(a_ref, w_ref, r_ref, o_ref):
    kk = pl.program_id(2)
    part = jnp.dot(a_ref[...], w_ref[...], preferred_element_type=F32)

    @pl.when(kk == 0)
    def _():
        o_ref[...] = r_ref[...] + part

    @pl.when(kk != 0)
    def _():
        o_ref[...] += part


def _matmul_residual(a, w, r, bk_target):
    m, k = a.shape
    n = w.shape[1]
    bm = _pick(m, 1024, BF16_SUBLANES)
    bn = _pick(n, 1024, V7X_LANES)
    bk = _pick(k, bk_target, V7X_LANES)
    blocks = [((bm, bk), BF16), ((bk, bn), BF16), ((bm, bn), F32), ((bm, bn), F32)]
    return pl.pallas_call(
        _matmul_residual_kernel,
        grid=(m // bm, n // bn, k // bk),
        in_specs=[
            pl.BlockSpec((bm, bk), lambda i, j, kk: (i, kk)),
            pl.BlockSpec((bk, bn), lambda i, j, kk: (kk, j)),
            pl.BlockSpec((bm, bn), lambda i, j, kk: (i, j)),
        ],
        out_specs=pl.BlockSpec((bm, bn), lambda i, j, kk: (i, j)),
        out_shape=jax.ShapeDtypeStruct((m, n), F32),
        compiler_params=_params(("arbitrary", "arbitrary", "arbitrary"), blocks),
        name="matmul_residual",
    )(a, w, r)


def _qkv_kernel(x_ref, g_ref, w_ref, qn_ref, kn_ref, o_ref, xn_ref, *, n_q_tiles, hd, q_scale):
    j = pl.program_id(1)

    @pl.when(j == 0)
    def _():
        xn_ref[...] = (_rms(x_ref[...], RMS_EPS) * g_ref[...]).astype(BF16)

    is_qk = j < 2 * n_q_tiles
    gain = jnp.where(j < n_q_tiles, qn_ref[...] * q_scale, kn_ref[...])
    x = xn_ref[...]
    for c in range(o_ref.shape[1] // (2 * hd)):
        acc = jnp.dot(x, w_ref[:, c * 2 * hd:(c + 1) * 2 * hd], preferred_element_type=F32)
        for mp in range(2):
            blk = acc[:, mp * hd:(mp + 1) * hd]
            cs = slice((2 * c + mp) * hd, (2 * c + mp + 1) * hd)
            o_ref[:, cs] = jnp.where(is_qk, _rms(blk, RMS_EPS) * gain, blk).astype(o_ref.dtype)


def _qkv_proj(h, g, w, q_norm, k_norm):
    m, d = h.shape
    n = w.shape[1]
    bm = _pick(m, 512, BF16_SUBLANES)
    bn = _pick(d, 1024, HEAD_DIM)
    kern = functools.partial(_qkv_kernel, n_q_tiles=d // bn, hd=HEAD_DIM, q_scale=HEAD_DIM ** -0.5)
    blocks = [((bm, d), F32), ((d, bn), BF16), ((bm, bn), BF16)]
    scratch = [((bm, d), BF16)]
    return pl.pallas_call(
        kern,
        grid=(m // bm, n // bn),
        in_specs=[
            pl.BlockSpec((bm, d), lambda i, j: (i, 0)),
            pl.BlockSpec((1, d), lambda i, j: (0, 0)),
            pl.BlockSpec((d, bn), lambda i, j: (0, j)),
            pl.BlockSpec((1, HEAD_DIM), lambda i, j: (0, 0)),
            pl.BlockSpec((1, HEAD_DIM), lambda i, j: (0, 0)),
        ],
        out_specs=pl.BlockSpec((bm, bn), lambda i, j: (i, j)),
        out_shape=jax.ShapeDtypeStruct((m, n), BF16),
        scratch_shapes=[pltpu.VMEM(s, t) for s, t in scratch],
        compiler_params=_params(("arbitrary", "arbitrary"), blocks, scratch),
        name="qkv_proj",
    )(h, g, w, q_norm, k_norm)


def _attn_kernel(q_ref, k_ref, v_ref, km_ref, vm_ref, lq1_ref, lk1_ref, lq2_ref, lk2_ref, sub_ref,
                 o_ref, acc_ref, m_ref, l_ref, *, tk, hd, n_meta, lambda_init):
    qi = pl.program_id(2)
    tq = q_ref.shape[0]
    nt = (((1,), (1,)), ((), ()))

    def update(mp, k, v, mask, first):
        q = q_ref[:, mp * hd:(mp + 1) * hd]
        s = lax.dot_general(q, k, nt, preferred_element_type=F32)
        if mask is not None:
            s = jnp.where(mask, s, MASK_VALUE)
        row_max = jnp.max(s, axis=1, keepdims=True)
        if first:
            m_new = row_max
        else:
            m_old = m_ref[mp][:, :1]
            m_new = jnp.maximum(m_old, row_max)
        p = jnp.exp(s - m_new)
        row_sum = jnp.sum(p, axis=1, keepdims=True)
        pv = jnp.dot(p.astype(BF16), v, preferred_element_type=F32)
        if first:
            l_new = row_sum
            acc_ref[mp] = pv
        else:
            alpha = jnp.exp(m_old - m_new)
            l_new = alpha * l_ref[mp][:, :1] + row_sum
            acc_ref[mp] = alpha * acc_ref[mp] + pv
        m_ref[mp] = jnp.broadcast_to(m_new, (tq, V7X_LANES))
        l_ref[mp] = jnp.broadcast_to(l_new, (tq, V7X_LANES))

    meta_mask = lax.broadcasted_iota(jnp.int32, (tq, km_ref.shape[0]), 1) < n_meta
    for mp in range(2):
        update(mp, km_ref[:, mp * hd:(mp + 1) * hd], vm_ref[...], meta_mask, True)

    n_full = (qi * tq) // tk

    def full_tile(kt, carry):
        k0 = pl.multiple_of(kt * tk, tk)
        for mp in range(2):
            update(mp, k_ref[pl.ds(k0, tk), mp * hd:(mp + 1) * hd], v_ref[pl.ds(k0, tk), :], None, False)
        return carry

    lax.fori_loop(0, n_full, full_tile, 0)

    row = lax.broadcasted_iota(jnp.int32, (tq, tk), 0)
    col = lax.broadcasted_iota(jnp.int32, (tq, tk), 1)
    for dt in range(tq // tk):
        k0 = pl.multiple_of(qi * tq + dt * tk, tk)
        diag_mask = col + dt * tk <= row
        for mp in range(2):
            update(mp, k_ref[pl.ds(k0, tk), mp * hd:(mp + 1) * hd], v_ref[pl.ds(k0, tk), :], diag_mask, False)

    lam = (jnp.exp(jnp.sum(lq1_ref[...] * lk1_ref[...], axis=-1, keepdims=True))
           - jnp.exp(jnp.sum(lq2_ref[...] * lk2_ref[...], axis=-1, keepdims=True)) + lambda_init)
    o1 = acc_ref[0] / l_ref[0][:, :1]
    o2 = acc_ref[1] / l_ref[1][:, :1]
    o = o1 - lam * o2
    o = _rms(o, SUBLN_EPS) * sub_ref[...] * (1.0 - lambda_init)
    o_ref[...] = o.astype(o_ref.dtype)


def _attn_bounded_kernel(shift_ref, q_ref, k_ref, v_ref, km_ref, vm_ref, lq1_ref, lk1_ref, lq2_ref, lk2_ref,
                         sub_ref, o_ref, kt_ref, acc_ref, ls_ref, *, hd, n_meta, lambda_init):
    qi = pl.program_id(2)
    tq = q_ref.shape[0]
    n_kt, _, tk = kt_ref.shape
    shift = shift_ref[0, 0]

    @pl.when(qi == 0)
    def _():
        def transpose_tile(c, carry):
            r0 = pl.multiple_of(c * tk, tk)
            kt_ref[c] = k_ref[pl.ds(r0, tk), :].T
            return carry

        lax.fori_loop(0, n_kt, transpose_tile, 0)

    all_rows = slice(0, tq)

    def accumulate(mp, rows, s, v, mask, first):
        p = jnp.exp(s - shift)
        if mask is not None:
            p = jnp.where(mask, p, 0.0)
        part = p[:, 0:V7X_LANES]
        for c in range(1, p.shape[1] // V7X_LANES):
            part = part + p[:, c * V7X_LANES:(c + 1) * V7X_LANES]
        pv = jnp.dot(p.astype(BF16), v, preferred_element_type=F32)
        if first:
            ls_ref[mp, rows] = part
            acc_ref[mp, rows] = pv
        else:
            ls_ref[mp, rows] += part
            acc_ref[mp, rows] += pv

    nt = (((1,), (1,)), ((), ()))
    meta_mask = lax.broadcasted_iota(jnp.int32, (tq, km_ref.shape[0]), 1) < n_meta
    for mp in range(2):
        hs = slice(mp * hd, (mp + 1) * hd)
        s = lax.dot_general(q_ref[:, hs], km_ref[:, hs], nt, preferred_element_type=F32)
        accumulate(mp, all_rows, s, vm_ref[...], meta_mask, True)

    def kv_tile(kt, rows, mask):
        k0 = pl.multiple_of(kt * tk, tk)
        v = v_ref[pl.ds(k0, tk), :]
        for mp in range(2):
            hs = slice(mp * hd, (mp + 1) * hd)
            s = jnp.dot(q_ref[rows, hs], kt_ref[kt, hs, :], preferred_element_type=F32)
            accumulate(mp, rows, s, v, mask, False)

    def two_full_tiles(pair, carry):
        kv_tile(2 * pair, all_rows, None)
        kv_tile(2 * pair + 1, all_rows, None)
        return carry

    lax.fori_loop(0, qi, two_full_tiles, 0)

    def causal(n_rows):
        row = lax.broadcasted_iota(jnp.int32, (n_rows, tk), 0)
        col = lax.broadcasted_iota(jnp.int32, (n_rows, tk), 1)
        return col <= row

    kv_tile(2 * qi, all_rows, causal(tq))
    kv_tile(2 * qi + 1, slice(tk, tq), causal(tq - tk))

    lam = (jnp.exp(jnp.sum(lq1_ref[...] * lk1_ref[...], axis=-1, keepdims=True))
           - jnp.exp(jnp.sum(lq2_ref[...] * lk2_ref[...], axis=-1, keepdims=True)) + lambda_init)
    o1 = acc_ref[0] / jnp.sum(ls_ref[0], axis=1, keepdims=True)
    o2 = acc_ref[1] / jnp.sum(ls_ref[1], axis=1, keepdims=True)
    o = o1 - lam * o2
    o = _rms(o, SUBLN_EPS) * sub_ref[...] * (1.0 - lambda_init)
    o_ref[...] = o.astype(o_ref.dtype)


SAFE_LOGIT_BOUND = 32.0


def _diff_attention(qkv, qkv_meta, logit_bound, lq1, lk1, lq2, lk2, subln, batch, n_meta, lambda_init):
    t, n3 = qkv.shape
    d = n3 // 3
    l = t // batch
    hw = 2 * HEAD_DIM
    heads = d // hw
    tq = _pick(l, 1024, 2 * V7X_LANES)
    tk = tq // 2
    nq = l // tq
    mrows = qkv_meta.shape[0]
    vec = pl.BlockSpec((1, HEAD_DIM), lambda b, h, i: (0, 0))
    blocks = [((tq, hw), BF16), ((l, hw), BF16), ((l, hw), BF16), ((mrows, hw), BF16), ((mrows, hw), BF16),
              ((tq, hw), BF16)]
    in_specs = [
        pl.BlockSpec((tq, hw), lambda b, h, i: (b * nq + i, h)),
        pl.BlockSpec((l, hw), lambda b, h, i: (b, heads + h)),
        pl.BlockSpec((l, hw), lambda b, h, i: (b, 2 * heads + h)),
        pl.BlockSpec((mrows, hw), lambda b, h, i: (0, heads + h)),
        pl.BlockSpec((mrows, hw), lambda b, h, i: (0, 2 * heads + h)),
        vec, vec, vec, vec,
        pl.BlockSpec((1, hw), lambda b, h, i: (0, 0)),
    ]
    common = dict(
        grid=(batch, heads, nq),
        out_specs=pl.BlockSpec((tq, hw), lambda b, h, i: (b * nq + i, h)),
        out_shape=jax.ShapeDtypeStruct((t, d), BF16),
    )
    operands = (qkv, qkv, qkv, qkv_meta, qkv_meta, lq1, lk1, lq2, lk2, subln)

    def running_max(_):
        kern = functools.partial(_attn_kernel, tk=tk, hd=HEAD_DIM, n_meta=n_meta, lambda_init=lambda_init)
        scratch = [((2, tq, hw), F32), ((2, tq, V7X_LANES), F32), ((2, tq, V7X_LANES), F32)]
        return pl.pallas_call(
            kern, in_specs=in_specs,
            scratch_shapes=[pltpu.VMEM(s, ty) for s, ty in scratch],
            compiler_params=_params(("arbitrary", "arbitrary", "arbitrary"), blocks, scratch),
            name="diff_attention_running_max", **common,
        )(*operands)

    def bounded(shift):
        kern = functools.partial(_attn_bounded_kernel, hd=HEAD_DIM, n_meta=n_meta, lambda_init=lambda_init)
        scratch = [((l // tk, hw, tk), BF16), ((2, tq, hw), F32), ((2, tq, V7X_LANES), F32)]
        return pl.pallas_call(
            kern,
            in_specs=[pl.BlockSpec((1, 1), lambda b, h, i: (0, 0), memory_space=pltpu.SMEM)] + in_specs,
            scratch_shapes=[pltpu.VMEM(s, ty) for s, ty in scratch],
            compiler_params=_params(("arbitrary", "arbitrary", "arbitrary"), blocks, scratch),
            name="diff_attention_bounded", **common,
        )(shift, *operands)

    shift = logit_bound.reshape(1, 1).astype(F32)
    return lax.cond(logit_bound <= SAFE_LOGIT_BOUND, bounded, running_max, shift)


def _router_kernel(x_ref, g_ref, r_ref, up_ref, rt_ref, *, n_exp):
    x = x_ref[...]
    u = _rms(x, RMS_EPS) * g_ref[...]
    u_hi = u.astype(BF16)
    u_lo = (u - u_hi.astype(F32)).astype(BF16)
    both = (jnp.dot(u_hi, r_ref[...], preferred_element_type=F32)
            + jnp.dot(u_lo, r_ref[...], preferred_element_type=F32))
    logits = both + pltpu.roll(both, V7X_LANES - n_exp, 1)
    lane = lax.broadcasted_iota(jnp.int32, logits.shape, 1).astype(F32)
    lg = jnp.where(lane < n_exp, logits, -jnp.inf)
    m1 = jnp.max(lg, axis=1, keepdims=True)
    i1 = jnp.min(jnp.where(lg == m1, lane, float(V7X_LANES)), axis=1, keepdims=True)
    lg2 = jnp.where(lane == i1, -jnp.inf, lg)
    m2 = jnp.max(lg2, axis=1, keepdims=True)
    i2 = jnp.min(jnp.where(lg2 == m2, lane, float(V7X_LANES)), axis=1, keepdims=True)
    e = jnp.exp(m2 - m1)
    g1 = 1.0 / (1.0 + e)
    g2 = e / (1.0 + e)
    rt = jnp.where(lane == 0, i1, jnp.where(lane == 1, i2, jnp.where(lane == 2, g1, jnp.where(lane == 3, g2, 0.0))))
    rt_ref[...] = rt
    bits = lax.bitcast_convert_type(u_hi.astype(F32), jnp.uint32)
    d2 = x.shape[1] // 2
    up_ref[...] = (bits[:, :d2] >> 16) | (bits[:, d2:] & jnp.uint32(0xFFFF0000))


def _router(h, g, router_split, n_exp):
    m, d = h.shape
    bm = _pick(m, 256, 8)
    kern = functools.partial(_router_kernel, n_exp=n_exp)
    blocks = [((bm, d), F32), ((d, V7X_LANES), BF16), ((bm, d // 2), jnp.uint32), ((bm, V7X_LANES), F32)]
    return pl.pallas_call(
        kern,
        grid=(m // bm,),
        in_specs=[
            pl.BlockSpec((bm, d), lambda i: (i, 0)),
            pl.BlockSpec((1, d), lambda i: (0, 0)),
            pl.BlockSpec((d, V7X_LANES), lambda i: (0, 0)),
        ],
        out_specs=[pl.BlockSpec((bm, d // 2), lambda i: (i, 0)), pl.BlockSpec((bm, V7X_LANES), lambda i: (i, 0))],
        out_shape=[jax.ShapeDtypeStruct((m, d // 2), jnp.uint32), jax.ShapeDtypeStruct((m, V7X_LANES), F32)],
        compiler_params=_params(("arbitrary",), blocks),
        name="router",
    )(h, g, router_split)


ROW_DMA_UNROLL = 8


def _dispatch_kernel(pos_ref, u_ref, xs_in_ref, xs_ref, sem):
    del xs_in_ref
    bm = u_ref.shape[0]

    def row_copy(r, s):
        return pltpu.make_async_copy(u_ref.at[pl.ds(r, 1)], xs_ref.at[pl.ds(pos_ref[0, 0, 2 * r + s], 1)], sem)

    def start(r, c):
        row_copy(r, 0).start(priority=0)
        row_copy(r, 1).start(priority=1)
        return c

    def wait(r, c):
        row_copy(r, 0).wait()
        row_copy(r, 1).wait()
        return c

    lax.fori_loop(0, bm, start, 0, unroll=ROW_DMA_UNROLL)
    lax.fori_loop(0, bm, wait, 0, unroll=ROW_DMA_UNROLL)


def _dispatch(up, pos_tiles, xs_zero):
    m, d2 = up.shape
    bm = pos_tiles.shape[2] // TOP_K
    blocks = [((bm, d2), jnp.uint32)]
    return pl.pallas_call(
        _dispatch_kernel,
        grid=(m // bm,),
        in_specs=[
            pl.BlockSpec((1, 1, TOP_K * bm), lambda i: (i, 0, 0), memory_space=pltpu.SMEM),
            pl.BlockSpec((bm, d2), lambda i: (i, 0)),
            pl.BlockSpec(memory_space=pl.ANY),
        ],
        out_specs=pl.BlockSpec(memory_space=pl.ANY),
        out_shape=jax.ShapeDtypeStruct(xs_zero.shape, xs_zero.dtype),
        scratch_shapes=[pltpu.SemaphoreType.DMA(())],
        input_output_aliases={2: 0},
        compiler_params=_params(("arbitrary",), blocks),
        name="moe_dispatch",
    )(pos_tiles, up, xs_zero)


def _combine_kernel(pos_ref, pos_next_ref, h_ref, rt_ref, y_ref, o_ref, buf_ref, sem):
    i = pl.program_id(0)
    bm = h_ref.shape[0]
    slot = i % 2

    def row_copy(p_ref, sl, r, s):
        return pltpu.make_async_copy(y_ref.at[pl.ds(p_ref[0, 0, 2 * r + s], 1)],
                                     buf_ref.at[sl, s, pl.ds(r, 1)], sem.at[sl])

    def issue(p_ref, sl):
        def start(r, c):
            row_copy(p_ref, sl, r, 0).start(priority=0)
            row_copy(p_ref, sl, r, 1).start(priority=1)
            return c

        lax.fori_loop(0, bm, start, 0, unroll=ROW_DMA_UNROLL)

    @pl.when(i == 0)
    def _():
        issue(pos_ref, 0)

    @pl.when(i + 1 < pl.num_programs(0))
    def _():
        issue(pos_next_ref, 1 - slot)

    def wait(r, c):
        row_copy(pos_ref, slot, r, 0).wait()
        row_copy(pos_ref, slot, r, 1).wait()
        return c

    lax.fori_loop(0, bm, wait, 0, unroll=ROW_DMA_UNROLL)
    rt = rt_ref[...]
    o_ref[...] = h_ref[...] + rt[:, 2:3] * buf_ref[slot, 0] + rt[:, 3:4] * buf_ref[slot, 1]


def _combine(h, route, y, pos_tiles):
    m, d = h.shape
    n_tiles = pos_tiles.shape[0]
    bm = pos_tiles.shape[2] // TOP_K
    blocks = [((bm, d), F32), ((bm, V7X_LANES), F32), ((bm, d), F32)]
    scratch = [((2, TOP_K, bm, d), F32)]
    pos_spec = lambda index: pl.BlockSpec((1, 1, TOP_K * bm), index, memory_space=pltpu.SMEM)
    return pl.pallas_call(
        _combine_kernel,
        grid=(n_tiles,),
        in_specs=[
            pos_spec(lambda i: (i, 0, 0)),
            pos_spec(lambda i: (jnp.minimum(i + 1, n_tiles - 1), 0, 0)),
            pl.BlockSpec((bm, d), lambda i: (i, 0)),
            pl.BlockSpec((bm, V7X_LANES), lambda i: (i, 0)),
            pl.BlockSpec(memory_space=pl.ANY),
        ],
        out_specs=pl.BlockSpec((bm, d), lambda i: (i, 0)),
        out_shape=jax.ShapeDtypeStruct((m, d), F32),
        scratch_shapes=[pltpu.VMEM(scratch[0][0], F32), pltpu.SemaphoreType.DMA((2,))],
        compiler_params=_params(("arbitrary",), blocks, scratch),
        name="moe_combine",
    )(pos_tiles, pos_tiles, h, route, y)


def _routing_tables(route, n_exp, tm, n_tiles):
    experts = route[:, :TOP_K].astype(jnp.int32).reshape(-1)
    onehot = (experts[:, None] == jnp.arange(n_exp, dtype=jnp.int32)[None, :]).astype(jnp.int32)
    csum = jnp.cumsum(onehot, axis=0)
    rank = jnp.sum(csum * onehot, axis=1) - 1
    counts = csum[-1]
    padded = ((counts + tm - 1) // tm) * tm
    ends = jnp.cumsum(padded)
    starts = ends - padded
    pos = jnp.sum(starts[None, :] * onehot, axis=1) + rank
    n_used = (ends[-1] // tm).astype(jnp.int32)
    tile_row = jnp.minimum(jnp.arange(n_tiles, dtype=jnp.int32), n_used - 1) * tm
    tile_expert = jnp.sum((tile_row[:, None] >= ends[None, :]).astype(jnp.int32), axis=1)
    return pos, jnp.minimum(tile_expert, n_exp - 1).astype(jnp.int32), n_used.reshape(1)


def kernel(x, meta_tokens, norm_mix, norm_ffn, pool_w, pool_scale, ffn_w_gate, ffn_w_up, ffn_w_down,
           w_qkv, q_norm, k_norm, lambda_q1, lambda_k1, lambda_q2, lambda_k2, subln, w_o,
           router, exp_w_gate, exp_w_up, exp_w_down):
    batch, seq, d = x.shape
    n_meta = meta_tokens.shape[0]
    n_exp = router.shape[-1]
    assert norm_mix.shape[0] == 2, "one pooling layer followed by one attention layer"
    halo = max(POOL_WINDOWS)
    assert n_meta >= halo and n_meta <= V7X_LANES
    t = batch * seq
    lambda_init = 0.8 - 0.6 * math.exp(-0.3 * 1)

    row = lambda v: v.reshape(1, -1).astype(F32)
    pool_wb = pool_w[0].astype(BF16)
    wg, wu = (w.reshape(w.shape[1:]).astype(F32) for w in (ffn_w_gate, ffn_w_up))
    wd = ffn_w_down[0].astype(BF16)
    wqkv, wo = w_qkv[0].astype(BF16), w_o[0].astype(BF16)
    ewg, ewu, ewd = (w.reshape(w.shape[1:]).astype(F32) for w in (exp_w_gate, exp_w_up, exp_w_down))
    r_hi = router[0].astype(BF16)
    r_lo = (router[0].astype(F32) - r_hi.astype(F32)).astype(BF16)
    router_split = jnp.pad(jnp.concatenate([r_hi, r_lo], axis=1), ((0, 0), (0, V7X_LANES - 2 * n_exp)))

    def layer0_and_qkv(tokens, first_rows, pos0):
        b, l, _ = tokens.shape
        h1, u2 = _pool_layer(tokens, first_rows, row(norm_mix[0]), pool_wb, row(pool_scale[0]),
                             row(norm_ffn[0]), pos0)
        a = _gateup(u2.reshape(b * l, d), wg, wu)
        h2 = _matmul_residual(a, wd, h1.reshape(b * l, d), 3584)
        qkv = _qkv_proj(h2, row(norm_mix[1]), wqkv, row(q_norm[0]), row(k_norm[0]))
        return h2, qkv

    meta = meta_tokens.astype(F32)
    _, qkv_meta = layer0_and_qkv(meta[None], jnp.zeros((halo, d), F32), 0)
    qkv_meta = jnp.pad(qkv_meta, ((0, V7X_LANES - n_meta), (0, 0)))
    h2, qkv = layer0_and_qkv(x, meta[n_meta - halo:], n_meta)

    logit_bound = 1.01 * math.sqrt(HEAD_DIM) * jnp.max(jnp.abs(q_norm[0])) * jnp.max(jnp.abs(k_norm[0]))
    o = _diff_attention(qkv, qkv_meta, logit_bound.astype(F32), row(lambda_q1[0]), row(lambda_k1[0]),
                        row(lambda_q2[0]), row(lambda_k2[0]), row(subln[0]), batch, n_meta, lambda_init)
    h3 = _matmul_residual(o, wo, h2, d)

    up, route = _router(h3, row(norm_ffn[1]), router_split, n_exp)
    tm = _pick(t, 512, BF16_SUBLANES)
    n_tiles = (TOP_K * t) // tm + n_exp
    pos, tile_expert, n_used = _routing_tables(route, n_exp, tm, n_tiles)
    bm_rows = _pick(t, 128, 8)
    pos_tiles = pos.reshape(t // bm_rows, 1, TOP_K * bm_rows)
    xs = _dispatch(up, pos_tiles, jnp.zeros((n_tiles * tm, d // 2), jnp.uint32))
    runs = _runs_from_tile_expert(tile_expert, n_used, n_exp)
    a = _staged_matmul(xs, (ewg, ewu), runs, tm, 512, BF16, True, "expert_gateup")
    y = _staged_matmul(a, (ewd,), runs, tm, 1024, F32, False, "expert_down")
    out = _combine(h3, route, y, pos_tiles)
    return out.reshape(batch, seq, d)
```

```python
import functools
import math

import jax
import jax.numpy as jnp
from jax import lax
from jax.experimental import pallas as pl
from jax.experimental.pallas import tpu as pltpu

POOL_WINDOWS = (2, 4, 8, 16)
HEAD_DIM = 128
TOP_K = 2
RMS_EPS = 1e-6
SUBLN_EPS = 1e-5
MASK_VALUE = -1e30

V7X_LANES = 128
V7X_VMEM_BYTES = 64 * 1024 * 1024
V7X_VMEM_RESERVE_BYTES = 6 * 1024 * 1024
BF16_SUBLANES = 16
F32_SUBLANES = 8

F32 = jnp.float32
BF16 = jnp.bfloat16


def _pick(dim, target, mult):
    for b in range(min(dim, target), 0, -1):
        if dim % b == 0 and b % mult == 0:
            return b
    return dim


def _nbytes(shape, dtype):
    return math.prod(shape) * jnp.dtype(dtype).itemsize


def _params(semantics, pipelined, scratch=()):
    est = 2 * sum(_nbytes(s, d) for s, d in pipelined) + sum(_nbytes(s, d) for s, d in scratch)
    limit = min(V7X_VMEM_BYTES - V7X_VMEM_RESERVE_BYTES, est + 16 * 1024 * 1024)
    return pltpu.CompilerParams(dimension_semantics=semantics, vmem_limit_bytes=limit)


def _rms(v, eps):
    return v * lax.rsqrt(jnp.mean(v * v, axis=-1, keepdims=True) + eps)


def _pool_kernel(x_ref, prev_ref, first_ref, gm_ref, pw_ref, ps_ref, gf_ref, h_ref, u_ref, ext_ref, lv_ref,
                 *, windows, halo, pos0):
    i = pl.program_id(1)
    tm = x_ref.shape[1]
    c = pw_ref.shape[1]
    pad = ext_ref.shape[0] - halo - tm
    top = pad + halo
    n = halo + tm
    x = x_ref[0]
    gm = gm_ref[...]
    before = jnp.where(i == 0, first_ref[...], prev_ref[0])
    ext_ref[0:pad] = jnp.zeros((pad, ext_ref.shape[1]), F32)
    lv_ref[:, 0:pad] = jnp.zeros((2, pad, c), F32)
    ext_ref[pad:top] = _rms(before, RMS_EPS) * gm
    ext_ref[top:] = _rms(x, RMS_EPS) * gm
    pos = pos0 + i * tm + lax.broadcasted_iota(jnp.int32, (tm, 1), 0)
    for g, w in enumerate(windows):
        cs = slice(g * c, (g + 1) * c)
        lv_ref[0, pad:] = ext_ref[pad:, cs] + ext_ref[pad - 1:pad - 1 + n, cs]
        cur, half = 0, 2
        while half < w:
            lv_ref[1 - cur, pad:] = lv_ref[cur, pad:] + lv_ref[cur, pad - half:pad - half + n]
            cur, half = 1 - cur, 2 * half
        inv_cnt = 1.0 / jnp.minimum(pos + 1, w).astype(F32)
        pooled = lv_ref[cur, top:] * inv_cnt - ext_ref[top:, cs]
        out = jnp.dot(pooled.astype(BF16), pw_ref[g], preferred_element_type=F32)
        h_ref[0, :, cs] = x[:, cs] + out * ps_ref[:, cs]
    h = h_ref[0]
    u_ref[0] = (_rms(h, RMS_EPS) * gf_ref[...]).astype(BF16)


def _pool_layer(x, first_rows, g_mix, pool_w, pool_scale, g_ffn, pos0):
    b, l, d = x.shape
    halo = max(POOL_WINDOWS)
    assert first_rows.shape == (halo, d)
    tm = _pick(l, 256, halo)
    nb = tm // halo
    g, c, _ = pool_w.shape
    kern = functools.partial(_pool_kernel, windows=POOL_WINDOWS, halo=halo, pos0=pos0)
    row = lambda bi, i: (bi, i, 0)
    const2 = lambda bi, i: (0, 0)
    blocks = [((1, tm, d), F32), ((1, halo, d), F32), ((halo, d), F32), ((g, c, c), BF16),
              ((1, tm, d), F32), ((1, tm, d), BF16)]
    assert all(w >= 2 and w & (w - 1) == 0 for w in POOL_WINDOWS)
    rows = F32_SUBLANES + halo + tm
    scratch = [((rows, d), F32), ((2, rows, c), F32)]
    return pl.pallas_call(
        kern,
        grid=(b, l // tm),
        in_specs=[
            pl.BlockSpec((1, tm, d), row),
            pl.BlockSpec((1, halo, d), lambda bi, i: (bi, jnp.maximum(i * nb - 1, 0), 0)),
            pl.BlockSpec((halo, d), const2),
            pl.BlockSpec((1, d), const2),
            pl.BlockSpec((g, c, c), lambda bi, i: (0, 0, 0)),
            pl.BlockSpec((1, d), const2),
            pl.BlockSpec((1, d), const2),
        ],
        out_specs=[pl.BlockSpec((1, tm, d), row), pl.BlockSpec((1, tm, d), row)],
        out_shape=[jax.ShapeDtypeStruct((b, l, d), F32), jax.ShapeDtypeStruct((b, l, d), BF16)],
        scratch_shapes=[pltpu.VMEM(s, t) for s, t in scratch],
        compiler_params=_params(("arbitrary", "arbitrary"), blocks, scratch),
        name="pool_layer",
    )(x, x, first_rows, g_mix, pool_w, pool_scale, g_ffn)


def _runs_from_tile_expert(tile_expert, n_used, n_exp):
    te = tile_expert
    prev = jnp.concatenate([te[:1] - 1, te[:-1]])
    start = (te != prev).astype(jnp.int32)
    ids = jnp.arange(n_exp, dtype=jnp.int32)
    present = jnp.any(te[None, :] == ids[:, None], axis=1)
    later = jnp.logical_and(present[None, :], ids[None, :] > ids[:, None])
    next_of = jnp.min(jnp.where(later, ids[None, :], n_exp), axis=1)
    next_of = jnp.where(next_of == n_exp, te[0], next_of).astype(jnp.int32)
    return te, n_used, start, next_of[te], (te == te[-1]).astype(jnp.int32)


def _staged_kernel(te_ref, nu_ref, st_ref, nx_ref, lr_ref, x_ref, *rest, n_w, packed):
    w_hbm = rest[:n_w]
    o_ref, stage_ref, wb_ref, sem = rest[n_w:]
    j = pl.program_id(0)
    i = pl.program_id(1)
    bn = o_ref.shape[1]
    used = i < nu_ref[0]

    def copies(e, jj):
        c0 = pl.multiple_of(jj * bn, bn)
        return [pltpu.make_async_copy(w.at[e, :, pl.ds(c0, bn)], stage_ref.at[n], sem.at[n])
                for n, w in enumerate(w_hbm)]

    @pl.when(jnp.logical_and(used, st_ref[i] == 1))
    def _():
        @pl.when(jnp.logical_and(j == 0, i == 0))
        def _():
            for c in copies(te_ref[0], 0):
                c.start()

        for c in copies(te_ref[i], j):
            c.wait()
        wb_ref[...] = stage_ref[...].astype(BF16)
        j_next = j + lr_ref[i]

        @pl.when(j_next < pl.num_programs(0))
        def _():
            for c in copies(nx_ref[i], j_next):
                c.start()

    @pl.when(used)
    def _():
        if packed:
            d2 = x_ref.shape[1]
            w = x_ref[...]
            parts = [(lax.bitcast_convert_type(w << 16, F32).astype(BF16), slice(0, d2)),
                     (lax.bitcast_convert_type(w & jnp.uint32(0xFFFF0000), F32).astype(BF16), slice(d2, 2 * d2))]
        else:
            parts = [(x_ref[...], slice(None))]
        prods = [sum(jnp.dot(xp, wb_ref[n, ks], preferred_element_type=F32) for xp, ks in parts)
                 for n in range(n_w)]
        if n_w == 2:
            g, u = prods
            o_ref[...] = (g * jax.nn.sigmoid(g) * u).astype(o_ref.dtype)
        else:
            o_ref[...] = prods[0].astype(o_ref.dtype)

    @pl.when(jnp.logical_not(used))
    def _():
        o_ref[...] = jnp.zeros_like(o_ref)


def _staged_matmul(x, weights, runs, tm, bn_target, out_dtype, packed, name):
    rows = x.shape[0]
    _, k, n = weights[0].shape
    n_w = len(weights)
    bn = _pick(n, bn_target, V7X_LANES)
    kern = functools.partial(_staged_kernel, n_w=n_w, packed=packed)
    blocks = [((tm, x.shape[1]), x.dtype), ((tm, bn), out_dtype)]
    scratch = [((n_w, k, bn), F32), ((n_w, k, bn), BF16)]
    grid_spec = pltpu.PrefetchScalarGridSpec(
        num_scalar_prefetch=5,
        grid=(n // bn, rows // tm),
        in_specs=[pl.BlockSpec((tm, x.shape[1]), lambda j, i, te, nu, *_: (jnp.minimum(i, nu[0] - 1), 0))]
        + [pl.BlockSpec(memory_space=pl.ANY)] * n_w,
        out_specs=pl.BlockSpec((tm, bn), lambda j, i, *_: (i, j)),
        scratch_shapes=[pltpu.VMEM(s, t) for s, t in scratch] + [pltpu.SemaphoreType.DMA((n_w,))],
    )
    return pl.pallas_call(
        kern,
        grid_spec=grid_spec,
        out_shape=jax.ShapeDtypeStruct((rows, n), out_dtype),
        compiler_params=_params(("arbitrary", "arbitrary"), blocks, scratch),
        name=name,
    )(*runs, x, *weights)


def _gateup(x, wg, wu):
    m = x.shape[0]
    bm = _pick(m, 1024, BF16_SUBLANES)
    n_tiles = m // bm
    runs = _runs_from_tile_expert(jnp.zeros((n_tiles,), jnp.int32), jnp.full((1,), n_tiles, jnp.int32), 1)
    return _staged_matmul(x, (wg[None], wu[None]), runs, bm, 512, BF16, False, "ffn_gateup")


def _matmul_residual_kernel(a_ref, w_ref, r_ref, o_ref):
    kk = pl.program_id(2)
    part = jnp.dot(a_ref[...], w_ref[...], preferred_element_type=F32)

    @pl.when(kk == 0)
    def _():
        o_ref[...] = r_ref[...] + part

    @pl.when(kk != 0)
    def _():
        o_ref[...] += part


def _matmul_residual(a, w, r, bk_target):
    m, k = a.shape
    n = w.shape[1]
    bm = _pick(m, 1024, BF16_SUBLANES)
    bn = _pick(n, 1024, V7X_LANES)
    bk = _pick(k, bk_target, V7X_LANES)
    blocks = [((bm, bk), BF16), ((bk, bn), BF16), ((bm, bn), F32), ((bm, bn), F32)]
    return pl.pallas_call(
        _matmul_residual_kernel,
        grid=(m // bm, n // bn, k // bk),
        in_specs=[
            pl.BlockSpec((bm, bk), lambda i, j, kk: (i, kk)),
            pl.BlockSpec((bk, bn), lambda i, j, kk: (kk, j)),
            pl.BlockSpec((bm, bn), lambda i, j, kk: (i, j)),
        ],
        out_specs=pl.BlockSpec((bm, bn), lambda i, j, kk: (i, j)),
        out_shape=jax.ShapeDtypeStruct((m, n), F32),
        compiler_params=_params(("arbitrary", "arbitrary", "arbitrary"), blocks),
        name="matmul_residual",
    )(a, w, r)


def _qkv_kernel(x_ref, g_ref, w_ref, qn_ref, kn_ref, o_ref, xn_ref, *, n_q_tiles, hd, q_scale):
    j = pl.program_id(1)

    @pl.when(j == 0)
    def _():
        xn_ref[...] = (_rms(x_ref[...], RMS_EPS) * g_ref[...]).astype(BF16)

    is_qk = j < 2 * n_q_tiles
    gain = jnp.where(j < n_q_tiles, qn_ref[...] * q_scale, kn_ref[...])
    x = xn_ref[...]
    for c in range(o_ref.shape[1] // (2 * hd)):
        acc = jnp.dot(x, w_ref[:, c * 2 * hd:(c + 1) * 2 * hd], preferred_element_type=F32)
        for mp in range(2):
            blk = acc[:, mp * hd:(mp + 1) * hd]
            cs = slice((2 * c + mp) * hd, (2 * c + mp + 1) * hd)
            o_ref[:, cs] = jnp.where(is_qk, _rms(blk, RMS_EPS) * gain, blk).astype(o_ref.dtype)


def _qkv_proj(h, g, w, q_norm, k_norm):
    m, d = h.shape
    n = w.shape[1]
    bm = _pick(m, 512, BF16_SUBLANES)
    bn = _pick(d, 1024, HEAD_DIM)
    kern = functools.partial(_qkv_kernel, n_q_tiles=d // bn, hd=HEAD_DIM, q_scale=HEAD_DIM ** -0.5)
    blocks = [((bm, d), F32), ((d, bn), BF16), ((bm, bn), BF16)]
    scratch = [((bm, d), BF16)]
    return pl.pallas_call(
        kern,
        grid=(m // bm, n // bn),
        in_specs=[
            pl.BlockSpec((bm, d), lambda i, j: (i, 0)),
            pl.BlockSpec((1, d), lambda i, j: (0, 0)),
            pl.BlockSpec((d, bn), lambda i, j: (0, j)),
            pl.BlockSpec((1, HEAD_DIM), lambda i, j: (0, 0)),
            pl.BlockSpec((1, HEAD_DIM), lambda i, j: (0, 0)),
        ],
        out_specs=pl.BlockSpec((bm, bn), lambda i, j: (i, j)),
        out_shape=jax.ShapeDtypeStruct((m, n), BF16),
        scratch_shapes=[pltpu.VMEM(s, t) for s, t in scratch],
        compiler_params=_params(("arbitrary", "arbitrary"), blocks, scratch),
        name="qkv_proj",
    )(h, g, w, q_norm, k_norm)


def _attn_kernel(q_ref, k_ref, v_ref, km_ref, vm_ref, lq1_ref, lk1_ref, lq2_ref, lk2_ref, sub_ref,
                 o_ref, acc_ref, m_ref, l_ref, *, tk, hd, n_meta, lambda_init):
    qi = pl.program_id(2)
    tq = q_ref.shape[0]
    nt = (((1,), (1,)), ((), ()))

    def update(mp, k, v, mask, first):
        q = q_ref[:, mp * hd:(mp + 1) * hd]
        s = lax.dot_general(q, k, nt, preferred_element_type=F32)
        if mask is not None:
            s = jnp.where(mask, s, MASK_VALUE)
        row_max = jnp.max(s, axis=1, keepdims=True)
        if first:
            m_new = row_max
        else:
            m_old = m_ref[mp][:, :1]
            m_new = jnp.maximum(m_old, row_max)
        p = jnp.exp(s - m_new)
        row_sum = jnp.sum(p, axis=1, keepdims=True)
        pv = jnp.dot(p.astype(BF16), v, preferred_element_type=F32)
        if first:
            l_new = row_sum
            acc_ref[mp] = pv
        else:
            alpha = jnp.exp(m_old - m_new)
            l_new = alpha * l_ref[mp][:, :1] + row_sum
            acc_ref[mp] = alpha * acc_ref[mp] + pv
        m_ref[mp] = jnp.broadcast_to(m_new, (tq, V7X_LANES))
        l_ref[mp] = jnp.broadcast_to(l_new, (tq, V7X_LANES))

    meta_mask = lax.broadcasted_iota(jnp.int32, (tq, km_ref.shape[0]), 1) < n_meta
    for mp in range(2):
        update(mp, km_ref[:, mp * hd:(mp + 1) * hd], vm_ref[...], meta_mask, True)

    n_full = (qi * tq) // tk

    def full_tile(kt, carry):
        k0 = pl.multiple_of(kt * tk, tk)
        for mp in range(2):
            update(mp, k_ref[pl.ds(k0, tk), mp * hd:(mp + 1) * hd], v_ref[pl.ds(k0, tk), :], None, False)
        return carry

    lax.fori_loop(0, n_full, full_tile, 0)

    row = lax.broadcasted_iota(jnp.int32, (tq, tk), 0)
    col = lax.broadcasted_iota(jnp.int32, (tq, tk), 1)
    for dt in range(tq // tk):
        k0 = pl.multiple_of(qi * tq + dt * tk, tk)
        diag_mask = col + dt * tk <= row
        for mp in range(2):
            update(mp, k_ref[pl.ds(k0, tk), mp * hd:(mp + 1) * hd], v_ref[pl.ds(k0, tk), :], diag_mask, False)

    lam = (jnp.exp(jnp.sum(lq1_ref[...] * lk1_ref[...], axis=-1, keepdims=True))
           - jnp.exp(jnp.sum(lq2_ref[...] * lk2_ref[...], axis=-1, keepdims=True)) + lambda_init)
    o1 = acc_ref[0] / l_ref[0][:, :1]
    o2 = acc_ref[1] / l_ref[1][:, :1]
    o = o1 - lam * o2
    o = _rms(o, SUBLN_EPS) * sub_ref[...] * (1.0 - lambda_init)
    o_ref[...] = o.astype(o_ref.dtype)


def _attn_bounded_kernel(shift_ref, q_ref, k_ref, v_ref, km_ref, vm_ref, lq1_ref, lk1_ref, lq2_ref, lk2_ref,
                         sub_ref, o_ref, kt_ref, acc_ref, ls_ref, *, hd, n_meta, lambda_init):
    qi = pl.program_id(2)
    tq = q_ref.shape[0]
    n_kt, _, tk = kt_ref.shape
    shift = shift_ref[0, 0]

    @pl.when(qi == 0)
    def _():
        def transpose_tile(c, carry):
            r0 = pl.multiple_of(c * tk, tk)
            kt_ref[c] = k_ref[pl.ds(r0, tk), :].T
            return carry

        lax.fori_loop(0, n_kt, transpose_tile, 0)

    all_rows = slice(0, tq)

    def accumulate(mp, rows, s, v, mask, first):
        p = jnp.exp(s - shift)
        if mask is not None:
            p = jnp.where(mask, p, 0.0)
        part = p[:, 0:V7X_LANES]
        for c in range(1, p.shape[1] // V7X_LANES):
            part = part + p[:, c * V7X_LANES:(c + 1) * V7X_LANES]
        pv = jnp.dot(p.astype(BF16), v, preferred_element_type=F32)
        if first:
            ls_ref[mp, rows] = part
            acc_ref[mp, rows] = pv
        else:
            ls_ref[mp, rows] += part
            acc_ref[mp, rows] += pv

    nt = (((1,), (1,)), ((), ()))
    meta_mask = lax.broadcasted_iota(jnp.int32, (tq, km_ref.shape[0]), 1) < n_meta
    for mp in range(2):
        hs = slice(mp * hd, (mp + 1) * hd)
        s = lax.dot_general(q_ref[:, hs], km_ref[:, hs], nt, preferred_element_type=F32)
        accumulate(mp, all_rows, s, vm_ref[...], meta_mask, True)

    def kv_tile(kt, rows, mask):
        k0 = pl.multiple_of(kt * tk, tk)
        v = v_ref[pl.ds(k0, tk), :]
        for mp in range(2):
            hs = slice(mp * hd, (mp + 1) * hd)
            s = jnp.dot(q_ref[rows, hs], kt_ref[kt, hs, :], preferred_element_type=F32)
            accumulate(mp, rows, s, v, mask, False)

    def full_tiles(first, count):
        for t in range(count):
            kv_tile(first + t, all_rows, None)

    def four_full_tiles(quad, carry):
        full_tiles(4 * quad, 4)
        return carry

    lax.fori_loop(0, qi // 2, four_full_tiles, 0)

    @pl.when(qi % 2 == 1)
    def _():
        full_tiles(2 * qi - 2, 2)

    def causal(n_rows):
        row = lax.broadcasted_iota(jnp.int32, (n_rows, tk), 0)
        col = lax.broadcasted_iota(jnp.int32, (n_rows, tk), 1)
        return col <= row

    kv_tile(2 * qi, all_rows, causal(tq))
    kv_tile(2 * qi + 1, slice(tk, tq), causal(tq - tk))

    lam = (jnp.exp(jnp.sum(lq1_ref[...] * lk1_ref[...], axis=-1, keepdims=True))
           - jnp.exp(jnp.sum(lq2_ref[...] * lk2_ref[...], axis=-1, keepdims=True)) + lambda_init)
    o1 = acc_ref[0] / jnp.sum(ls_ref[0], axis=1, keepdims=True)
    o2 = acc_ref[1] / jnp.sum(ls_ref[1], axis=1, keepdims=True)
    o = o1 - lam * o2
    o = _rms(o, SUBLN_EPS) * sub_ref[...] * (1.0 - lambda_init)
    o_ref[...] = o.astype(o_ref.dtype)


SAFE_LOGIT_BOUND = 32.0


def _diff_attention(qkv, qkv_meta, logit_bound, lq1, lk1, lq2, lk2, subln, batch, n_meta, lambda_init):
    t, n3 = qkv.shape
    d = n3 // 3
    l = t // batch
    hw = 2 * HEAD_DIM
    heads = d // hw
    tq = _pick(l, 1024, 2 * V7X_LANES)
    tk = tq // 2
    nq = l // tq
    mrows = qkv_meta.shape[0]
    vec = pl.BlockSpec((1, HEAD_DIM), lambda b, h, i: (0, 0))
    blocks = [((tq, hw), BF16), ((l, hw), BF16), ((l, hw), BF16), ((mrows, hw), BF16), ((mrows, hw), BF16),
              ((tq, hw), BF16)]
    in_specs = [
        pl.BlockSpec((tq, hw), lambda b, h, i: (b * nq + i, h)),
        pl.BlockSpec((l, hw), lambda b, h, i: (b, heads + h)),
        pl.BlockSpec((l, hw), lambda b, h, i: (b, 2 * heads + h)),
        pl.BlockSpec((mrows, hw), lambda b, h, i: (0, heads + h)),
        pl.BlockSpec((mrows, hw), lambda b, h, i: (0, 2 * heads + h)),
        vec, vec, vec, vec,
        pl.BlockSpec((1, hw), lambda b, h, i: (0, 0)),
    ]
    common = dict(
        grid=(batch, heads, nq),
        out_specs=pl.BlockSpec((tq, hw), lambda b, h, i: (b * nq + i, h)),
        out_shape=jax.ShapeDtypeStruct((t, d), BF16),
    )
    operands = (qkv, qkv, qkv, qkv_meta, qkv_meta, lq1, lk1, lq2, lk2, subln)

    def running_max(_):
        kern = functools.partial(_attn_kernel, tk=tk, hd=HEAD_DIM, n_meta=n_meta, lambda_init=lambda_init)
        scratch = [((2, tq, hw), F32), ((2, tq, V7X_LANES), F32), ((2, tq, V7X_LANES), F32)]
        return pl.pallas_call(
            kern, in_specs=in_specs,
            scratch_shapes=[pltpu.VMEM(s, ty) for s, ty in scratch],
            compiler_params=_params(("arbitrary", "arbitrary", "arbitrary"), blocks, scratch),
            name="diff_attention_running_max", **common,
        )(*operands)

    def bounded(shift):
        kern = functools.partial(_attn_bounded_kernel, hd=HEAD_DIM, n_meta=n_meta, lambda_init=lambda_init)
        scratch = [((l // tk, hw, tk), BF16), ((2, tq, hw), F32), ((2, tq, V7X_LANES), F32)]
        return pl.pallas_call(
            kern,
            in_specs=[pl.BlockSpec((1, 1), lambda b, h, i: (0, 0), memory_space=pltpu.SMEM)] + in_specs,
            scratch_shapes=[pltpu.VMEM(s, ty) for s, ty in scratch],
            compiler_params=_params(("arbitrary", "arbitrary", "arbitrary"), blocks, scratch),
            name="diff_attention_bounded", **common,
        )(shift, *operands)

    shift = logit_bound.reshape(1, 1).astype(F32)
    return lax.cond(logit_bound <= SAFE_LOGIT_BOUND, bounded, running_max, shift)


def _router_kernel(x_ref, g_ref, r_ref, up_ref, rt_ref, *, n_exp):
    x = x_ref[...]
    u = _rms(x, RMS_EPS) * g_ref[...]
    u_hi = u.astype(BF16)
    u_lo = (u - u_hi.astype(F32)).astype(BF16)
    both = (jnp.dot(u_hi, r_ref[...], preferred_element_type=F32)
            + jnp.dot(u_lo, r_ref[...], preferred_element_type=F32))
    logits = both + pltpu.roll(both, V7X_LANES - n_exp, 1)
    lane = lax.broadcasted_iota(jnp.int32, logits.shape, 1).astype(F32)
    lg = jnp.where(lane < n_exp, logits, -jnp.inf)
    m1 = jnp.max(lg, axis=1, keepdims=True)
    i1 = jnp.min(jnp.where(lg == m1, lane, float(V7X_LANES)), axis=1, keepdims=True)
    lg2 = jnp.where(lane == i1, -jnp.inf, lg)
    m2 = jnp.max(lg2, axis=1, keepdims=True)
    i2 = jnp.min(jnp.where(lg2 == m2, lane, float(V7X_LANES)), axis=1, keepdims=True)
    e = jnp.exp(m2 - m1)
    g1 = 1.0 / (1.0 + e)
    g2 = e / (1.0 + e)
    rt = jnp.where(lane == 0, i1, jnp.where(lane == 1, i2, jnp.where(lane == 2, g1, jnp.where(lane == 3, g2, 0.0))))
    rt_ref[...] = rt
    bits = lax.bitcast_convert_type(u_hi.astype(F32), jnp.uint32)
    d2 = x.shape[1] // 2
    up_ref[...] = (bits[:, :d2] >> 16) | (bits[:, d2:] & jnp.uint32(0xFFFF0000))


def _router(h, g, router_split, n_exp):
    m, d = h.shape
    bm = _pick(m, 256, 8)
    kern = functools.partial(_router_kernel, n_exp=n_exp)
    blocks = [((bm, d), F32), ((d, V7X_LANES), BF16), ((bm, d // 2), jnp.uint32), ((bm, V7X_LANES), F32)]
    return pl.pallas_call(
        kern,
        grid=(m // bm,),
        in_specs=[
            pl.BlockSpec((bm, d), lambda i: (i, 0)),
            pl.BlockSpec((1, d), lambda i: (0, 0)),
            pl.BlockSpec((d, V7X_LANES), lambda i: (0, 0)),
        ],
        out_specs=[pl.BlockSpec((bm, d // 2), lambda i: (i, 0)), pl.BlockSpec((bm, V7X_LANES), lambda i: (i, 0))],
        out_shape=[jax.ShapeDtypeStruct((m, d // 2), jnp.uint32), jax.ShapeDtypeStruct((m, V7X_LANES), F32)],
        compiler_params=_params(("arbitrary",), blocks),
        name="router",
    )(h, g, router_split)


ROW_DMA_UNROLL = 8


def _dispatch_kernel(pos_ref, u_ref, xs_in_ref, xs_ref, sem):
    del xs_in_ref
    bm = u_ref.shape[0]

    def row_copy(r, s):
        return pltpu.make_async_copy(u_ref.at[pl.ds(r, 1)], xs_ref.at[pl.ds(pos_ref[0, 0, 2 * r + s], 1)], sem)

    def start(r, c):
        row_copy(r, 0).start(priority=0)
        row_copy(r, 1).start(priority=1)
        return c

    def wait(r, c):
        row_copy(r, 0).wait()
        row_copy(r, 1).wait()
        return c

    lax.fori_loop(0, bm, start, 0, unroll=ROW_DMA_UNROLL)
    lax.fori_loop(0, bm, wait, 0, unroll=ROW_DMA_UNROLL)


def _dispatch(up, pos_tiles, xs_zero):
    m, d2 = up.shape
    bm = pos_tiles.shape[2] // TOP_K
    blocks = [((bm, d2), jnp.uint32)]
    return pl.pallas_call(
        _dispatch_kernel,
        grid=(m // bm,),
        in_specs=[
            pl.BlockSpec((1, 1, TOP_K * bm), lambda i: (i, 0, 0), memory_space=pltpu.SMEM),
            pl.BlockSpec((bm, d2), lambda i: (i, 0)),
            pl.BlockSpec(memory_space=pl.ANY),
        ],
        out_specs=pl.BlockSpec(memory_space=pl.ANY),
        out_shape=jax.ShapeDtypeStruct(xs_zero.shape, xs_zero.dtype),
        scratch_shapes=[pltpu.SemaphoreType.DMA(())],
        input_output_aliases={2: 0},
        compiler_params=_params(("arbitrary",), blocks),
        name="moe_dispatch",
    )(pos_tiles, up, xs_zero)


def _combine_kernel(pos_ref, pos_next_ref, h_ref, rt_ref, y_ref, o_ref, buf_ref, sem):
    i = pl.program_id(0)
    bm = h_ref.shape[0]
    slot = i % 2

    def row_copy(p_ref, sl, r, s):
        return pltpu.make_async_copy(y_ref.at[pl.ds(p_ref[0, 0, 2 * r + s], 1)],
                                     buf_ref.at[sl, s, pl.ds(r, 1)], sem.at[sl])

    def issue(p_ref, sl):
        def start(r, c):
            row_copy(p_ref, sl, r, 0).start(priority=0)
            row_copy(p_ref, sl, r, 1).start(priority=1)
            return c

        lax.fori_loop(0, bm, start, 0, unroll=ROW_DMA_UNROLL)

    @pl.when(i == 0)
    def _():
        issue(pos_ref, 0)

    @pl.when(i + 1 < pl.num_programs(0))
    def _():
        issue(pos_next_ref, 1 - slot)

    def wait(r, c):
        row_copy(pos_ref, slot, r, 0).wait()
        row_copy(pos_ref, slot, r, 1).wait()
        return c

    lax.fori_loop(0, bm, wait, 0, unroll=ROW_DMA_UNROLL)
    rt = rt_ref[...]
    o_ref[...] = h_ref[...] + rt[:, 2:3] * buf_ref[slot, 0] + rt[:, 3:4] * buf_ref[slot, 1]


def _combine(h, route, y, pos_tiles):
    m, d = h.shape
    n_tiles = pos_tiles.shape[0]
    bm = pos_tiles.shape[2] // TOP_K
    blocks = [((bm, d), F32), ((bm, V7X_LANES), F32), ((bm, d), F32)]
    scratch = [((2, TOP_K, bm, d), F32)]
    pos_spec = lambda index: pl.BlockSpec((1, 1, TOP_K * bm), index, memory_space=pltpu.SMEM)
    return pl.pallas_call(
        _combine_kernel,
        grid=(n_tiles,),
        in_specs=[
            pos_spec(lambda i: (i, 0, 0)),
            pos_spec(lambda i: (jnp.minimum(i + 1, n_tiles - 1), 0, 0)),
            pl.BlockSpec((bm, d), lambda i: (i, 0)),
            pl.BlockSpec((bm, V7X_LANES), lambda i: (i, 0)),
            pl.BlockSpec(memory_space=pl.ANY),
        ],
        out_specs=pl.BlockSpec((bm, d), lambda i: (i, 0)),
        out_shape=jax.ShapeDtypeStruct((m, d), F32),
        scratch_shapes=[pltpu.VMEM(scratch[0][0], F32), pltpu.SemaphoreType.DMA((2,))],
        compiler_params=_params(("arbitrary",), blocks, scratch),
        name="moe_combine",
    )(pos_tiles, pos_tiles, h, route, y)


def _routing_tables(route, n_exp, tm, n_tiles):
    experts = route[:, :TOP_K].astype(jnp.int32).reshape(-1)
    onehot = (experts[:, None] == jnp.arange(n_exp, dtype=jnp.int32)[None, :]).astype(jnp.int32)
    csum = jnp.cumsum(onehot, axis=0)
    rank = jnp.sum(csum * onehot, axis=1) - 1
    counts = csum[-1]
    padded = ((counts + tm - 1) // tm) * tm
    ends = jnp.cumsum(padded)
    starts = ends - padded
    pos = jnp.sum(starts[None, :] * onehot, axis=1) + rank
    n_used = (ends[-1] // tm).astype(jnp.int32)
    tile_row = jnp.minimum(jnp.arange(n_tiles, dtype=jnp.int32), n_used - 1) * tm
    tile_expert = jnp.sum((tile_row[:, None] >= ends[None, :]).astype(jnp.int32), axis=1)
    return pos, jnp.minimum(tile_expert, n_exp - 1).astype(jnp.int32), n_used.reshape(1)


def kernel(x, meta_tokens, norm_mix, norm_ffn, pool_w, pool_scale, ffn_w_gate, ffn_w_up, ffn_w_down,
           w_qkv, q_norm, k_norm, lambda_q1, lambda_k1, lambda_q2, lambda_k2, subln, w_o,
           router, exp_w_gate, exp_w_up, exp_w_down):
    batch, seq, d = x.shape
    n_meta = meta_tokens.shape[0]
    n_exp = router.shape[-1]
    assert norm_mix.shape[0] == 2, "one pooling layer followed by one attention layer"
    halo = max(POOL_WINDOWS)
    assert n_meta >= halo and n_meta <= V7X_LANES
    t = batch * seq
    lambda_init = 0.8 - 0.6 * math.exp(-0.3 * 1)

    row = lambda v: v.reshape(1, -1).astype(F32)
    pool_wb = pool_w[0].astype(BF16)
    wg, wu = (w.reshape(w.shape[1:]).astype(F32) for w in (ffn_w_gate, ffn_w_up))
    wd = ffn_w_down[0].astype(BF16)
    wqkv, wo = w_qkv[0].astype(BF16), w_o[0].astype(BF16)
    ewg, ewu, ewd = (w.reshape(w.shape[1:]).astype(F32) for w in (exp_w_gate, exp_w_up, exp_w_down))
    r_hi = router[0].astype(BF16)
    r_lo = (router[0].astype(F32) - r_hi.astype(F32)).astype(BF16)
    router_split = jnp.pad(jnp.concatenate([r_hi, r_lo], axis=1), ((0, 0), (0, V7X_LANES - 2 * n_exp)))

    def layer0_and_qkv(tokens, first_rows, pos0):
        b, l, _ = tokens.shape
        h1, u2 = _pool_layer(tokens, first_rows, row(norm_mix[0]), pool_wb, row(pool_scale[0]),
                             row(norm_ffn[0]), pos0)
        a = _gateup(u2.reshape(b * l, d), wg, wu)
        h2 = _matmul_residual(a, wd, h1.reshape(b * l, d), 3584)
        qkv = _qkv_proj(h2, row(norm_mix[1]), wqkv, row(q_norm[0]), row(k_norm[0]))
        return h2, qkv

    meta = meta_tokens.astype(F32)
    _, qkv_meta = layer0_and_qkv(meta[None], jnp.zeros((halo, d), F32), 0)
    qkv_meta = jnp.pad(qkv_meta, ((0, V7X_LANES - n_meta), (0, 0)))
    h2, qkv = layer0_and_qkv(x, meta[n_meta - halo:], n_meta)

    logit_bound = 1.01 * math.sqrt(HEAD_DIM) * jnp.max(jnp.abs(q_norm[0])) * jnp.max(jnp.abs(k_norm[0]))
    o = _diff_attention(qkv, qkv_meta, logit_bound.astype(F32), row(lambda_q1[0]), row(lambda_k1[0]),
                        row(lambda_q2[0]), row(lambda_k2[0]), row(subln[0]), batch, n_meta, lambda_init)
    h3 = _matmul_residual(o, wo, h2, d)

    up, route = _router(h3, row(norm_ffn[1]), router_split, n_exp)
    tm = _pick(t, 512, BF16_SUBLANES)
    n_tiles = (TOP_K * t) // tm + n_exp
    pos, tile_expert, n_used = _routing_tables(route, n_exp, tm, n_tiles)
    bm_rows = _pick(t, 128, 8)
    pos_tiles = pos.reshape(t // bm_rows, 1, TOP_K * bm_rows)
    xs = _dispatch(up, pos_tiles, jnp.zeros((n_tiles * tm, d // 2), jnp.uint32))
    runs = _runs_from_tile_expert(tile_expert, n_used, n_exp)
    a = _staged_matmul(xs, (ewg, ewu), runs, tm, 512, BF16, True, "expert_gateup")
    y = _staged_matmul(a, (ewd,), runs, tm, 1024, F32, False, "expert_down")
    out = _combine(h3, route, y, pos_tiles)
    return out.reshape(batch, seq, d)
```

```python
import functools
import math

import jax
import jax.numpy as jnp
from jax import lax
from jax.experimental import pallas as pl
from jax.experimental.pallas import tpu as pltpu

POOL_WINDOWS = (2, 4, 8, 16)
HEAD_DIM = 128
TOP_K = 2
RMS_EPS = 1e-6
SUBLN_EPS = 1e-5
MASK_VALUE = -1e30

V7X_LANES = 128
V7X_VMEM_BYTES = 64 * 1024 * 1024
V7X_VMEM_RESERVE_BYTES = 6 * 1024 * 1024
BF16_SUBLANES = 16
F32_SUBLANES = 8

F32 = jnp.float32
BF16 = jnp.bfloat16


def _pick(dim, target, mult):
    for b in range(min(dim, target), 0, -1):
        if dim % b == 0 and b % mult == 0:
            return b
    return dim


def _nbytes(shape, dtype):
    return math.prod(shape) * jnp.dtype(dtype).itemsize


def _params(semantics, pipelined, scratch=()):
    est = 2 * sum(_nbytes(s, d) for s, d in pipelined) + sum(_nbytes(s, d) for s, d in scratch)
    limit = min(V7X_VMEM_BYTES - V7X_VMEM_RESERVE_BYTES, est + 16 * 1024 * 1024)
    return pltpu.CompilerParams(dimension_semantics=semantics, vmem_limit_bytes=limit)


def _rms(v, eps):
    return v * lax.rsqrt(jnp.mean(v * v, axis=-1, keepdims=True) + eps)


def _pack_bf16_pairs(v):
    half = v.shape[1] // 2
    bits = lax.bitcast_convert_type(v.astype(BF16).astype(F32), jnp.uint32)
    return (bits[:, :half] >> 16) | (bits[:, half:] & jnp.uint32(0xFFFF0000))


def _unpack_bf16_pairs(w):
    return (lax.bitcast_convert_type(w << 16, F32),
            lax.bitcast_convert_type(w & jnp.uint32(0xFFFF0000), F32))


def _pool_kernel(x_ref, prev_ref, first_ref, gm_ref, pw_ref, ps_ref, gf_ref, h_ref, u_ref, ext_ref, lv_ref,
                 *, windows, halo, pos0):
    i = pl.program_id(1)
    tm = x_ref.shape[1]
    c = pw_ref.shape[1]
    pad = ext_ref.shape[0] - halo - tm
    top = pad + halo
    n = halo + tm
    x = x_ref[0]
    gm = gm_ref[...]
    before = jnp.where(i == 0, first_ref[...], prev_ref[0])
    ext_ref[0:pad] = jnp.zeros((pad, ext_ref.shape[1]), F32)
    lv_ref[:, 0:pad] = jnp.zeros((2, pad, c), F32)
    ext_ref[pad:top] = _rms(before, RMS_EPS) * gm
    ext_ref[top:] = _rms(x, RMS_EPS) * gm
    pos = pos0 + i * tm + lax.broadcasted_iota(jnp.int32, (tm, 1), 0)
    for g, w in enumerate(windows):
        cs = slice(g * c, (g + 1) * c)
        lv_ref[0, pad:] = ext_ref[pad:, cs] + ext_ref[pad - 1:pad - 1 + n, cs]
        cur, half = 0, 2
        while half < w:
            lv_ref[1 - cur, pad:] = lv_ref[cur, pad:] + lv_ref[cur, pad - half:pad - half + n]
            cur, half = 1 - cur, 2 * half
        inv_cnt = 1.0 / jnp.minimum(pos + 1, w).astype(F32)
        pooled = lv_ref[cur, top:] * inv_cnt - ext_ref[top:, cs]
        out = jnp.dot(pooled.astype(BF16), pw_ref[g], preferred_element_type=F32)
        h_ref[0, :, cs] = x[:, cs] + out * ps_ref[:, cs]
    h = h_ref[0]
    u_ref[0] = (_rms(h, RMS_EPS) * gf_ref[...]).astype(BF16)


def _pool_layer(x, first_rows, g_mix, pool_w, pool_scale, g_ffn, pos0):
    b, l, d = x.shape
    halo = max(POOL_WINDOWS)
    assert first_rows.shape == (halo, d)
    tm = _pick(l, 256, halo)
    nb = tm // halo
    g, c, _ = pool_w.shape
    kern = functools.partial(_pool_kernel, windows=POOL_WINDOWS, halo=halo, pos0=pos0)
    row = lambda bi, i: (bi, i, 0)
    const2 = lambda bi, i: (0, 0)
    blocks = [((1, tm, d), F32), ((1, halo, d), F32), ((halo, d), F32), ((g, c, c), BF16),
              ((1, tm, d), F32), ((1, tm, d), BF16)]
    assert all(w >= 2 and w & (w - 1) == 0 for w in POOL_WINDOWS)
    rows = F32_SUBLANES + halo + tm
    scratch = [((rows, d), F32), ((2, rows, c), F32)]
    return pl.pallas_call(
        kern,
        grid=(b, l // tm),
        in_specs=[
            pl.BlockSpec((1, tm, d), row),
            pl.BlockSpec((1, halo, d), lambda bi, i: (bi, jnp.maximum(i * nb - 1, 0), 0)),
            pl.BlockSpec((halo, d), const2),
            pl.BlockSpec((1, d), const2),
            pl.BlockSpec((g, c, c), lambda bi, i: (0, 0, 0)),
            pl.BlockSpec((1, d), const2),
            pl.BlockSpec((1, d), const2),
        ],
        out_specs=[pl.BlockSpec((1, tm, d), row), pl.BlockSpec((1, tm, d), row)],
        out_shape=[jax.ShapeDtypeStruct((b, l, d), F32), jax.ShapeDtypeStruct((b, l, d), BF16)],
        scratch_shapes=[pltpu.VMEM(s, t) for s, t in scratch],
        compiler_params=_params(("arbitrary", "arbitrary"), blocks, scratch),
        name="pool_layer",
    )(x, x, first_rows, g_mix, pool_w, pool_scale, g_ffn)


def _runs_from_tile_expert(tile_expert, n_used, n_exp):
    te = tile_expert
    prev = jnp.concatenate([te[:1] - 1, te[:-1]])
    start = (te != prev).astype(jnp.int32)
    ids = jnp.arange(n_exp, dtype=jnp.int32)
    present = jnp.any(te[None, :] == ids[:, None], axis=1)
    later = jnp.logical_and(present[None, :], ids[None, :] > ids[:, None])
    next_of = jnp.min(jnp.where(later, ids[None, :], n_exp), axis=1)
    next_of = jnp.where(next_of == n_exp, te[0], next_of).astype(jnp.int32)
    return te, n_used, start, next_of[te], (te == te[-1]).astype(jnp.int32)


def _staged_kernel(te_ref, nu_ref, st_ref, nx_ref, lr_ref, x_ref, *rest, n_w, packed):
    w_hbm = rest[:n_w]
    o_ref, stage_ref, wb_ref, sem = rest[n_w:]
    j = pl.program_id(0)
    i = pl.program_id(1)
    bn = stage_ref.shape[2]
    used = i < nu_ref[0]

    def copies(e, jj):
        c0 = pl.multiple_of(jj * bn, bn)
        return [pltpu.make_async_copy(w.at[e, :, pl.ds(c0, bn)], stage_ref.at[n], sem.at[n])
                for n, w in enumerate(w_hbm)]

    @pl.when(jnp.logical_and(used, st_ref[i] == 1))
    def _():
        @pl.when(jnp.logical_and(j == 0, i == 0))
        def _():
            for c in copies(te_ref[0], 0):
                c.start()

        for c in copies(te_ref[i], j):
            c.wait()
        wb_ref[...] = stage_ref[...].astype(BF16)
        j_next = j + lr_ref[i]

        @pl.when(j_next < pl.num_programs(0))
        def _():
            for c in copies(nx_ref[i], j_next):
                c.start()

    @pl.when(used)
    def _():
        if packed:
            d2 = x_ref.shape[1]
            lo, hi = _unpack_bf16_pairs(x_ref[...])
            parts = [(lo.astype(BF16), slice(0, d2)), (hi.astype(BF16), slice(d2, 2 * d2))]
        else:
            parts = [(x_ref[...], slice(None))]
        prods = [sum(jnp.dot(xp, wb_ref[n, ks], preferred_element_type=F32) for xp, ks in parts)
                 for n in range(n_w)]
        if n_w == 2:
            g, u = prods
            o_ref[...] = (g * jax.nn.sigmoid(g) * u).astype(o_ref.dtype)
        else:
            o_ref[...] = _pack_bf16_pairs(prods[0])

    @pl.when(jnp.logical_not(used))
    def _():
        o_ref[...] = jnp.zeros_like(o_ref)


def _staged_matmul(x, weights, runs, tm, bn_target, packed, name):
    rows = x.shape[0]
    _, k, n = weights[0].shape
    n_w = len(weights)
    bn = _pick(n, bn_target, 2 * V7X_LANES)
    kern = functools.partial(_staged_kernel, n_w=n_w, packed=packed)
    out_cols, out_bn, out_dtype = (n, bn, BF16) if n_w == 2 else (n // 2, bn // 2, jnp.uint32)
    blocks = [((tm, x.shape[1]), x.dtype), ((tm, out_bn), out_dtype)]
    scratch = [((n_w, k, bn), F32), ((n_w, k, bn), BF16)]
    grid_spec = pltpu.PrefetchScalarGridSpec(
        num_scalar_prefetch=5,
        grid=(n // bn, rows // tm),
        in_specs=[pl.BlockSpec((tm, x.shape[1]), lambda j, i, te, nu, *_: (jnp.minimum(i, nu[0] - 1), 0))]
        + [pl.BlockSpec(memory_space=pl.ANY)] * n_w,
        out_specs=pl.BlockSpec((tm, out_bn), lambda j, i, *_: (i, j)),
        scratch_shapes=[pltpu.VMEM(s, t) for s, t in scratch] + [pltpu.SemaphoreType.DMA((n_w,))],
    )
    out = pl.pallas_call(
        kern,
        grid_spec=grid_spec,
        out_shape=jax.ShapeDtypeStruct((rows, out_cols), out_dtype),
        compiler_params=_params(("arbitrary", "arbitrary"), blocks, scratch),
        name=name,
    )(*runs, x, *weights)
    return out if n_w == 2 else (out, bn)


def _gateup(x, wg, wu):
    m = x.shape[0]
    bm = _pick(m, 1024, BF16_SUBLANES)
    n_tiles = m // bm
    runs = _runs_from_tile_expert(jnp.zeros((n_tiles,), jnp.int32), jnp.full((1,), n_tiles, jnp.int32), 1)
    return _staged_matmul(x, (wg[None], wu[None]), runs, bm, 512, False, "ffn_gateup")


def _matmul_residual_kernel(a_ref, w_ref, r_ref, o_ref):
    kk = pl.program_id(2)
    part = jnp.dot(a_ref[...], w_ref[...], preferred_element_type=F32)

    @pl.when(kk == 0)
    def _():
        o_ref[...] = r_ref[...] + part

    @pl.when(kk != 0)
    def _():
        o_ref[...] += part


def _matmul_residual(a, w, r, bk_target):
    m, k = a.shape
    n = w.shape[1]
    bm = _pick(m, 1024, BF16_SUBLANES)
    bn = _pick(n, 1024, V7X_LANES)
    bk = _pick(k, bk_target, V7X_LANES)
    blocks = [((bm, bk), BF16), ((bk, bn), BF16), ((bm, bn), F32), ((bm, bn), F32)]
    return pl.pallas_call(
        _matmul_residual_kernel,
        grid=(m // bm, n // bn, k // bk),
        in_specs=[
            pl.BlockSpec((bm, bk), lambda i, j, kk: (i, kk)),
            pl.BlockSpec((bk, bn), lambda i, j, kk: (kk, j)),
            pl.BlockSpec((bm, bn), lambda i, j, kk: (i, j)),
        ],
        out_specs=pl.BlockSpec((bm, bn), lambda i, j, kk: (i, j)),
        out_shape=jax.ShapeDtypeStruct((m, n), F32),
        compiler_params=_params(("arbitrary", "arbitrary", "arbitrary"), blocks),
        name="matmul_residual",
    )(a, w, r)


def _qkv_kernel(x_ref, g_ref, w_ref, qn_ref, kn_ref, o_ref, xn_ref, *, n_q_tiles, hd, q_scale):
    j = pl.program_id(1)

    @pl.when(j == 0)
    def _():
        xn_ref[...] = (_rms(x_ref[...], RMS_EPS) * g_ref[...]).astype(BF16)

    is_qk = j < 2 * n_q_tiles
    gain = jnp.where(j < n_q_tiles, qn_ref[...] * q_scale, kn_ref[...])
    x = xn_ref[...]
    for c in range(o_ref.shape[1] // (2 * hd)):
        acc = jnp.dot(x, w_ref[:, c * 2 * hd:(c + 1) * 2 * hd], preferred_element_type=F32)
        for mp in range(2):
            blk = acc[:, mp * hd:(mp + 1) * hd]
            cs = slice((2 * c + mp) * hd, (2 * c + mp + 1) * hd)
            o_ref[:, cs] = jnp.where(is_qk, _rms(blk, RMS_EPS) * gain, blk).astype(o_ref.dtype)


def _qkv_proj(h, g, w, q_norm, k_norm):
    m, d = h.shape
    n = w.shape[1]
    bm = _pick(m, 512, BF16_SUBLANES)
    bn = _pick(d, 1024, HEAD_DIM)
    kern = functools.partial(_qkv_kernel, n_q_tiles=d // bn, hd=HEAD_DIM, q_scale=HEAD_DIM ** -0.5)
    blocks = [((bm, d), F32), ((d, bn), BF16), ((bm, bn), BF16)]
    scratch = [((bm, d), BF16)]
    return pl.pallas_call(
        kern,
        grid=(m // bm, n // bn),
        in_specs=[
            pl.BlockSpec((bm, d), lambda i, j: (i, 0)),
            pl.BlockSpec((1, d), lambda i, j: (0, 0)),
            pl.BlockSpec((d, bn), lambda i, j: (0, j)),
            pl.BlockSpec((1, HEAD_DIM), lambda i, j: (0, 0)),
            pl.BlockSpec((1, HEAD_DIM), lambda i, j: (0, 0)),
        ],
        out_specs=pl.BlockSpec((bm, bn), lambda i, j: (i, j)),
        out_shape=jax.ShapeDtypeStruct((m, n), BF16),
        scratch_shapes=[pltpu.VMEM(s, t) for s, t in scratch],
        compiler_params=_params(("arbitrary", "arbitrary"), blocks, scratch),
        name="qkv_proj",
    )(h, g, w, q_norm, k_norm)


def _attn_kernel(q_ref, k_ref, v_ref, km_ref, vm_ref, lq1_ref, lk1_ref, lq2_ref, lk2_ref, sub_ref,
                 o_ref, acc_ref, m_ref, l_ref, *, tk, hd, n_meta, lambda_init):
    qi = pl.program_id(2)
    tq = q_ref.shape[0]
    nt = (((1,), (1,)), ((), ()))

    def update(mp, k, v, mask, first):
        q = q_ref[:, mp * hd:(mp + 1) * hd]
        s = lax.dot_general(q, k, nt, preferred_element_type=F32)
        if mask is not None:
            s = jnp.where(mask, s, MASK_VALUE)
        row_max = jnp.max(s, axis=1, keepdims=True)
        if first:
            m_new = row_max
        else:
            m_old = m_ref[mp][:, :1]
            m_new = jnp.maximum(m_old, row_max)
        p = jnp.exp(s - m_new)
        row_sum = jnp.sum(p, axis=1, keepdims=True)
        pv = jnp.dot(p.astype(BF16), v, preferred_element_type=F32)
        if first:
            l_new = row_sum
            acc_ref[mp] = pv
        else:
            alpha = jnp.exp(m_old - m_new)
            l_new = alpha * l_ref[mp][:, :1] + row_sum
            acc_ref[mp] = alpha * acc_ref[mp] + pv
        m_ref[mp] = jnp.broadcast_to(m_new, (tq, V7X_LANES))
        l_ref[mp] = jnp.broadcast_to(l_new, (tq, V7X_LANES))

    meta_mask = lax.broadcasted_iota(jnp.int32, (tq, km_ref.shape[0]), 1) < n_meta
    for mp in range(2):
        update(mp, km_ref[:, mp * hd:(mp + 1) * hd], vm_ref[...], meta_mask, True)

    n_full = (qi * tq) // tk

    def full_tile(kt, carry):
        k0 = pl.multiple_of(kt * tk, tk)
        for mp in range(2):
            update(mp, k_ref[pl.ds(k0, tk), mp * hd:(mp + 1) * hd], v_ref[pl.ds(k0, tk), :], None, False)
        return carry

    lax.fori_loop(0, n_full, full_tile, 0)

    row = lax.broadcasted_iota(jnp.int32, (tq, tk), 0)
    col = lax.broadcasted_iota(jnp.int32, (tq, tk), 1)
    for dt in range(tq // tk):
        k0 = pl.multiple_of(qi * tq + dt * tk, tk)
        diag_mask = col + dt * tk <= row
        for mp in range(2):
            update(mp, k_ref[pl.ds(k0, tk), mp * hd:(mp + 1) * hd], v_ref[pl.ds(k0, tk), :], diag_mask, False)

    lam = (jnp.exp(jnp.sum(lq1_ref[...] * lk1_ref[...], axis=-1, keepdims=True))
           - jnp.exp(jnp.sum(lq2_ref[...] * lk2_ref[...], axis=-1, keepdims=True)) + lambda_init)
    o1 = acc_ref[0] / l_ref[0][:, :1]
    o2 = acc_ref[1] / l_ref[1][:, :1]
    o = o1 - lam * o2
    o = _rms(o, SUBLN_EPS) * sub_ref[...] * (1.0 - lambda_init)
    o_ref[...] = o.astype(o_ref.dtype)


def _attn_bounded_kernel(shift_ref, q_ref, k_ref, v_ref, km_ref, vm_ref, lq1_ref, lk1_ref, lq2_ref, lk2_ref,
                         sub_ref, o_ref, kt_ref, acc_ref, ls_ref, *, hd, n_meta, lambda_init):
    qi = pl.program_id(2)
    tq = q_ref.shape[0]
    n_kt, _, tk = kt_ref.shape
    shift = shift_ref[0, 0]

    @pl.when(qi == 0)
    def _():
        def transpose_tile(c, carry):
            r0 = pl.multiple_of(c * tk, tk)
            kt_ref[c] = k_ref[pl.ds(r0, tk), :].T
            return carry

        lax.fori_loop(0, n_kt, transpose_tile, 0)

    all_rows = slice(0, tq)

    def accumulate(mp, rows, s, v, mask, first):
        p = jnp.exp(s - shift)
        if mask is not None:
            p = jnp.where(mask, p, 0.0)
        part = p[:, 0:V7X_LANES]
        for c in range(1, p.shape[1] // V7X_LANES):
            part = part + p[:, c * V7X_LANES:(c + 1) * V7X_LANES]
        pv = jnp.dot(p.astype(BF16), v, preferred_element_type=F32)
        if first:
            ls_ref[mp, rows] = part
            acc_ref[mp, rows] = pv
        else:
            ls_ref[mp, rows] += part
            acc_ref[mp, rows] += pv

    def kv_tile(kt, rows, mask, first=False):
        k0 = pl.multiple_of(kt * tk, tk)
        v = v_ref[pl.ds(k0, tk), :]
        for mp in range(2):
            hs = slice(mp * hd, (mp + 1) * hd)
            s = jnp.dot(q_ref[rows, hs], kt_ref[kt, hs, :], preferred_element_type=F32)
            accumulate(mp, rows, s, v, mask, first)

    def causal(n_rows):
        row = lax.broadcasted_iota(jnp.int32, (n_rows, tk), 0)
        col = lax.broadcasted_iota(jnp.int32, (n_rows, tk), 1)
        return col <= row

    kv_tile(2 * qi, all_rows, causal(tq), first=True)

    def full_tiles(first, count):
        for t in range(count):
            kv_tile(first + t, all_rows, None)

    def four_full_tiles(quad, carry):
        full_tiles(4 * quad, 4)
        return carry

    lax.fori_loop(0, qi // 2, four_full_tiles, 0)

    @pl.when(qi % 2 == 1)
    def _():
        full_tiles(2 * qi - 2, 2)

    nt = (((1,), (1,)), ((), ()))
    meta_mask = lax.broadcasted_iota(jnp.int32, (tq, km_ref.shape[0]), 1) < n_meta
    for mp in range(2):
        hs = slice(mp * hd, (mp + 1) * hd)
        s = lax.dot_general(q_ref[:, hs], km_ref[:, hs], nt, preferred_element_type=F32)
        accumulate(mp, all_rows, s, vm_ref[...], meta_mask, False)
    kv_tile(2 * qi + 1, slice(tk, tq), causal(tq - tk))

    lam = (jnp.exp(jnp.sum(lq1_ref[...] * lk1_ref[...], axis=-1, keepdims=True))
           - jnp.exp(jnp.sum(lq2_ref[...] * lk2_ref[...], axis=-1, keepdims=True)) + lambda_init)
    o1 = acc_ref[0] / jnp.sum(ls_ref[0], axis=1, keepdims=True)
    o2 = acc_ref[1] / jnp.sum(ls_ref[1], axis=1, keepdims=True)
    o = o1 - lam * o2
    o = _rms(o, SUBLN_EPS) * sub_ref[...] * (1.0 - lambda_init)
    o_ref[...] = o.astype(o_ref.dtype)


SAFE_LOGIT_BOUND = 32.0


def _diff_attention(qkv, qkv_meta, logit_bound, lq1, lk1, lq2, lk2, subln, batch, n_meta, lambda_init):
    t, n3 = qkv.shape
    d = n3 // 3
    l = t // batch
    hw = 2 * HEAD_DIM
    heads = d // hw
    tq = _pick(l, 1024, 2 * V7X_LANES)
    tk = tq // 2
    nq = l // tq
    mrows = qkv_meta.shape[0]
    vec = pl.BlockSpec((1, HEAD_DIM), lambda b, h, i: (0, 0))
    blocks = [((tq, hw), BF16), ((l, hw), BF16), ((l, hw), BF16), ((mrows, hw), BF16), ((mrows, hw), BF16),
              ((tq, hw), BF16)]
    in_specs = [
        pl.BlockSpec((tq, hw), lambda b, h, i: (b * nq + i, h)),
        pl.BlockSpec((l, hw), lambda b, h, i: (b, heads + h)),
        pl.BlockSpec((l, hw), lambda b, h, i: (b, 2 * heads + h)),
        pl.BlockSpec((mrows, hw), lambda b, h, i: (0, heads + h)),
        pl.BlockSpec((mrows, hw), lambda b, h, i: (0, 2 * heads + h)),
        vec, vec, vec, vec,
        pl.BlockSpec((1, hw), lambda b, h, i: (0, 0)),
    ]
    common = dict(
        grid=(batch, heads, nq),
        out_specs=pl.BlockSpec((tq, hw), lambda b, h, i: (b * nq + i, h)),
        out_shape=jax.ShapeDtypeStruct((t, d), BF16),
    )
    operands = (qkv, qkv, qkv, qkv_meta, qkv_meta, lq1, lk1, lq2, lk2, subln)

    def running_max(_):
        kern = functools.partial(_attn_kernel, tk=tk, hd=HEAD_DIM, n_meta=n_meta, lambda_init=lambda_init)
        scratch = [((2, tq, hw), F32), ((2, tq, V7X_LANES), F32), ((2, tq, V7X_LANES), F32)]
        return pl.pallas_call(
            kern, in_specs=in_specs,
            scratch_shapes=[pltpu.VMEM(s, ty) for s, ty in scratch],
            compiler_params=_params(("arbitrary", "arbitrary", "arbitrary"), blocks, scratch),
            name="diff_attention_running_max", **common,
        )(*operands)

    def bounded(shift):
        kern = functools.partial(_attn_bounded_kernel, hd=HEAD_DIM, n_meta=n_meta, lambda_init=lambda_init)
        scratch = [((l // tk, hw, tk), BF16), ((2, tq, hw), F32), ((2, tq, V7X_LANES), F32)]
        return pl.pallas_call(
            kern,
            in_specs=[pl.BlockSpec((1, 1), lambda b, h, i: (0, 0), memory_space=pltpu.SMEM)] + in_specs,
            scratch_shapes=[pltpu.VMEM(s, ty) for s, ty in scratch],
            compiler_params=_params(("arbitrary", "arbitrary", "arbitrary"), blocks, scratch),
            name="diff_attention_bounded", **common,
        )(shift, *operands)

    shift = logit_bound.reshape(1, 1).astype(F32)
    return lax.cond(logit_bound <= SAFE_LOGIT_BOUND, bounded, running_max, shift)


def _router_kernel(x_ref, g_ref, r_ref, up_ref, rt_ref, *, n_exp):
    x = x_ref[...]
    u = _rms(x, RMS_EPS) * g_ref[...]
    u_hi = u.astype(BF16)
    u_lo = (u - u_hi.astype(F32)).astype(BF16)
    both = (jnp.dot(u_hi, r_ref[...], preferred_element_type=F32)
            + jnp.dot(u_lo, r_ref[...], preferred_element_type=F32))
    logits = both + pltpu.roll(both, V7X_LANES - n_exp, 1)
    lane = lax.broadcasted_iota(jnp.int32, logits.shape, 1).astype(F32)
    lg = jnp.where(lane < n_exp, logits, -jnp.inf)
    m1 = jnp.max(lg, axis=1, keepdims=True)
    i1 = jnp.min(jnp.where(lg == m1, lane, float(V7X_LANES)), axis=1, keepdims=True)
    lg2 = jnp.where(lane == i1, -jnp.inf, lg)
    m2 = jnp.max(lg2, axis=1, keepdims=True)
    i2 = jnp.min(jnp.where(lg2 == m2, lane, float(V7X_LANES)), axis=1, keepdims=True)
    e = jnp.exp(m2 - m1)
    g1 = 1.0 / (1.0 + e)
    g2 = e / (1.0 + e)
    rt = jnp.where(lane == 0, i1, jnp.where(lane == 1, i2, jnp.where(lane == 2, g1, jnp.where(lane == 3, g2, 0.0))))
    rt_ref[...] = rt
    up_ref[...] = _pack_bf16_pairs(u)


def _router(h, g, router_split, n_exp):
    m, d = h.shape
    bm = _pick(m, 256, 8)
    kern = functools.partial(_router_kernel, n_exp=n_exp)
    blocks = [((bm, d), F32), ((d, V7X_LANES), BF16), ((bm, d // 2), jnp.uint32), ((bm, V7X_LANES), F32)]
    return pl.pallas_call(
        kern,
        grid=(m // bm,),
        in_specs=[
            pl.BlockSpec((bm, d), lambda i: (i, 0)),
            pl.BlockSpec((1, d), lambda i: (0, 0)),
            pl.BlockSpec((d, V7X_LANES), lambda i: (0, 0)),
        ],
        out_specs=[pl.BlockSpec((bm, d // 2), lambda i: (i, 0)), pl.BlockSpec((bm, V7X_LANES), lambda i: (i, 0))],
        out_shape=[jax.ShapeDtypeStruct((m, d // 2), jnp.uint32), jax.ShapeDtypeStruct((m, V7X_LANES), F32)],
        compiler_params=_params(("arbitrary",), blocks),
        name="router",
    )(h, g, router_split)


ROW_DMA_UNROLL = 8


def _dispatch_kernel(pos_ref, u_ref, xs_in_ref, xs_ref, sem):
    del xs_in_ref
    bm = u_ref.shape[0]

    def row_copy(r, s):
        return pltpu.make_async_copy(u_ref.at[pl.ds(r, 1)], xs_ref.at[pl.ds(pos_ref[0, 0, 2 * r + s], 1)], sem)

    def start(r, c):
        row_copy(r, 0).start(priority=0)
        row_copy(r, 1).start(priority=1)
        return c

    def wait(r, c):
        row_copy(r, 0).wait()
        row_copy(r, 1).wait()
        return c

    lax.fori_loop(0, bm, start, 0, unroll=ROW_DMA_UNROLL)
    lax.fori_loop(0, bm, wait, 0, unroll=ROW_DMA_UNROLL)


def _dispatch(up, pos_tiles, xs_zero):
    m, d2 = up.shape
    bm = pos_tiles.shape[2] // TOP_K
    blocks = [((bm, d2), jnp.uint32)]
    return pl.pallas_call(
        _dispatch_kernel,
        grid=(m // bm,),
        in_specs=[
            pl.BlockSpec((1, 1, TOP_K * bm), lambda i: (i, 0, 0), memory_space=pltpu.SMEM),
            pl.BlockSpec((bm, d2), lambda i: (i, 0)),
            pl.BlockSpec(memory_space=pl.ANY),
        ],
        out_specs=pl.BlockSpec(memory_space=pl.ANY),
        out_shape=jax.ShapeDtypeStruct(xs_zero.shape, xs_zero.dtype),
        scratch_shapes=[pltpu.SemaphoreType.DMA(())],
        input_output_aliases={2: 0},
        compiler_params=_params(("arbitrary",), blocks),
        name="moe_dispatch",
    )(pos_tiles, up, xs_zero)


def _combine_kernel(pos_ref, pos_next_ref, h_ref, rt_ref, y_ref, o_ref, buf_ref, sem, *, y_block):
    i = pl.program_id(0)
    bm = h_ref.shape[0]
    slot = i % 2

    def row_copy(p_ref, sl, r, s):
        return pltpu.make_async_copy(y_ref.at[pl.ds(p_ref[0, 0, 2 * r + s], 1)],
                                     buf_ref.at[sl, s, pl.ds(r, 1)], sem.at[sl])

    def start(p_ref, sl, r):
        row_copy(p_ref, sl, r, 0).start(priority=0)
        row_copy(p_ref, sl, r, 1).start(priority=1)

    def wait_all(p_ref, sl):
        def wait(r, c):
            row_copy(p_ref, sl, r, 0).wait()
            row_copy(p_ref, sl, r, 1).wait()
            return c

        lax.fori_loop(0, bm, wait, 0, unroll=ROW_DMA_UNROLL)

    @pl.when(i == 0)
    def _():
        lax.fori_loop(0, bm, lambda r, c: (start(pos_ref, 0, r), c)[1], 0, unroll=ROW_DMA_UNROLL)

    wait_all(pos_ref, slot)
    for r in range(bm):
        start(pos_next_ref, 1 - slot, r)
    rt = rt_ref[...]
    g1, g2 = rt[:, 2:3], rt[:, 3:4]
    half = y_block // 2
    for blk in range(h_ref.shape[1] // y_block):
        words = slice(blk * half, (blk + 1) * half)
        lo1, hi1 = _unpack_bf16_pairs(buf_ref[slot, 0, :, words])
        lo2, hi2 = _unpack_bf16_pairs(buf_ref[slot, 1, :, words])
        c_lo = slice(blk * y_block, blk * y_block + half)
        c_hi = slice(blk * y_block + half, (blk + 1) * y_block)
        o_ref[:, c_lo] = h_ref[:, c_lo] + g1 * lo1 + g2 * lo2
        o_ref[:, c_hi] = h_ref[:, c_hi] + g1 * hi1 + g2 * hi2

    @pl.when(i == pl.num_programs(0) - 1)
    def _():
        wait_all(pos_next_ref, 1 - slot)


def _combine(h, route, y, y_block, pos_tiles):
    m, d = h.shape
    n_tiles = pos_tiles.shape[0]
    bm = pos_tiles.shape[2] // TOP_K
    blocks = [((bm, d), F32), ((bm, V7X_LANES), F32), ((bm, d), F32)]
    scratch = [((2, TOP_K, bm, d // 2), jnp.uint32)]
    pos_spec = lambda index: pl.BlockSpec((1, 1, TOP_K * bm), index, memory_space=pltpu.SMEM)
    return pl.pallas_call(
        functools.partial(_combine_kernel, y_block=y_block),
        grid=(n_tiles,),
        in_specs=[
            pos_spec(lambda i: (i, 0, 0)),
            pos_spec(lambda i: (jnp.minimum(i + 1, n_tiles - 1), 0, 0)),
            pl.BlockSpec((bm, d), lambda i: (i, 0)),
            pl.BlockSpec((bm, V7X_LANES), lambda i: (i, 0)),
            pl.BlockSpec(memory_space=pl.ANY),
        ],
        out_specs=pl.BlockSpec((bm, d), lambda i: (i, 0)),
        out_shape=jax.ShapeDtypeStruct((m, d), F32),
        scratch_shapes=[pltpu.VMEM(*scratch[0]), pltpu.SemaphoreType.DMA((2,))],
        compiler_params=_params(("arbitrary",), blocks, scratch),
        name="moe_combine",
    )(pos_tiles, pos_tiles, h, route, y)


def _routing_tables(route, n_exp, tm, n_tiles):
    experts = route[:, :TOP_K].astype(jnp.int32).reshape(-1)
    onehot = (experts[:, None] == jnp.arange(n_exp, dtype=jnp.int32)[None, :]).astype(jnp.int32)
    csum = jnp.cumsum(onehot, axis=0)
    rank = jnp.sum(csum * onehot, axis=1) - 1
    counts = csum[-1]
    padded = ((counts + tm - 1) // tm) * tm
    ends = jnp.cumsum(padded)
    starts = ends - padded
    pos = jnp.sum(starts[None, :] * onehot, axis=1) + rank
    n_used = (ends[-1] // tm).astype(jnp.int32)
    tile_row = jnp.minimum(jnp.arange(n_tiles, dtype=jnp.int32), n_used - 1) * tm
    tile_expert = jnp.sum((tile_row[:, None] >= ends[None, :]).astype(jnp.int32), axis=1)
    return pos, jnp.minimum(tile_expert, n_exp - 1).astype(jnp.int32), n_used.reshape(1)


def kernel(x, meta_tokens, norm_mix, norm_ffn, pool_w, pool_scale, ffn_w_gate, ffn_w_up, ffn_w_down,
           w_qkv, q_norm, k_norm, lambda_q1, lambda_k1, lambda_q2, lambda_k2, subln, w_o,
           router, exp_w_gate, exp_w_up, exp_w_down):
    batch, seq, d = x.shape
    n_meta = meta_tokens.shape[0]
    n_exp = router.shape[-1]
    assert norm_mix.shape[0] == 2, "one pooling layer followed by one attention layer"
    halo = max(POOL_WINDOWS)
    assert n_meta >= halo and n_meta <= V7X_LANES
    t = batch * seq
    lambda_init = 0.8 - 0.6 * math.exp(-0.3 * 1)

    row = lambda v: v.reshape(1, -1).astype(F32)
    pool_wb = pool_w[0].astype(BF16)
    wg, wu = (w.reshape(w.shape[1:]).astype(F32) for w in (ffn_w_gate, ffn_w_up))
    wd = ffn_w_down[0].astype(BF16)
    wqkv, wo = w_qkv[0].astype(BF16), w_o[0].astype(BF16)
    ewg, ewu, ewd = (w.reshape(w.shape[1:]).astype(F32) for w in (exp_w_gate, exp_w_up, exp_w_down))
    r_hi = router[0].astype(BF16)
    r_lo = (router[0].astype(F32) - r_hi.astype(F32)).astype(BF16)
    router_split = jnp.pad(jnp.concatenate([r_hi, r_lo], axis=1), ((0, 0), (0, V7X_LANES - 2 * n_exp)))

    def layer0_and_qkv(tokens, first_rows, pos0):
        b, l, _ = tokens.shape
        h1, u2 = _pool_layer(tokens, first_rows, row(norm_mix[0]), pool_wb, row(pool_scale[0]),
                             row(norm_ffn[0]), pos0)
        a = _gateup(u2.reshape(b * l, d), wg, wu)
        h2 = _matmul_residual(a, wd, h1.reshape(b * l, d), 3584)
        qkv = _qkv_proj(h2, row(norm_mix[1]), wqkv, row(q_norm[0]), row(k_norm[0]))
        return h2, qkv

    meta = meta_tokens.astype(F32)
    _, qkv_meta = layer0_and_qkv(meta[None], jnp.zeros((halo, d), F32), 0)
    qkv_meta = jnp.pad(qkv_meta, ((0, V7X_LANES - n_meta), (0, 0)))
    h2, qkv = layer0_and_qkv(x, meta[n_meta - halo:], n_meta)

    logit_bound = 1.01 * math.sqrt(HEAD_DIM) * jnp.max(jnp.abs(q_norm[0])) * jnp.max(jnp.abs(k_norm[0]))
    o = _diff_attention(qkv, qkv_meta, logit_bound.astype(F32), row(lambda_q1[0]), row(lambda_k1[0]),
                        row(lambda_q2[0]), row(lambda_k2[0]), row(subln[0]), batch, n_meta, lambda_init)
    h3 = _matmul_residual(o, wo, h2, d)

    up, route = _router(h3, row(norm_ffn[1]), router_split, n_exp)
    tm = _pick(t, 512, BF16_SUBLANES)
    n_tiles = (TOP_K * t) // tm + n_exp
    pos, tile_expert, n_used = _routing_tables(route, n_exp, tm, n_tiles)
    bm_rows = _pick(t, 256, 8)
    pos_tiles = pos.reshape(t // bm_rows, 1, TOP_K * bm_rows)
    xs = _dispatch(up, pos_tiles, jnp.zeros((n_tiles * tm, d // 2), jnp.uint32))
    runs = _runs_from_tile_expert(tile_expert, n_used, n_exp)
    a = _staged_matmul(xs, (ewg, ewu), runs, tm, 512, True, "expert_gateup")
    y, y_block = _staged_matmul(a, (ewd,), runs, tm, 1024, False, "expert_down")
    out = _combine(h3, route, y, y_block, pos_tiles)
    return out.reshape(batch, seq, d)
```

```python
import functools
import math

import jax
import jax.numpy as jnp
from jax import lax
from jax.experimental import pallas as pl
from jax.experimental.pallas import tpu as pltpu

POOL_WINDOWS = (2, 4, 8, 16)
HEAD_DIM = 128
TOP_K = 2
RMS_EPS = 1e-6
SUBLN_EPS = 1e-5
MASK_VALUE = -1e30

V7X_LANES = 128
V7X_VMEM_BYTES = 64 * 1024 * 1024
V7X_VMEM_RESERVE_BYTES = 6 * 1024 * 1024
BF16_SUBLANES = 16
F32_SUBLANES = 8

F32 = jnp.float32
BF16 = jnp.bfloat16


def _pick(dim, target, mult):
    for b in range(min(dim, target), 0, -1):
        if dim % b == 0 and b % mult == 0:
            return b
    return dim


def _nbytes(shape, dtype):
    return math.prod(shape) * jnp.dtype(dtype).itemsize


def _params(semantics, pipelined, scratch=()):
    est = 2 * sum(_nbytes(s, d) for s, d in pipelined) + sum(_nbytes(s, d) for s, d in scratch)
    limit = min(V7X_VMEM_BYTES - V7X_VMEM_RESERVE_BYTES, est + 16 * 1024 * 1024)
    return pltpu.CompilerParams(dimension_semantics=semantics, vmem_limit_bytes=limit)


def _rms(v, eps):
    return v * lax.rsqrt(jnp.mean(v * v, axis=-1, keepdims=True) + eps)


def _pack_bf16_pairs(v):
    half = v.shape[1] // 2
    bits = lax.bitcast_convert_type(v.astype(BF16).astype(F32), jnp.uint32)
    return (bits[:, :half] >> 16) | (bits[:, half:] & jnp.uint32(0xFFFF0000))


def _unpack_bf16_pairs(w):
    return (lax.bitcast_convert_type(w << 16, F32),
            lax.bitcast_convert_type(w & jnp.uint32(0xFFFF0000), F32))


def _pool_kernel(x_ref, prev_ref, first_ref, gm_ref, pw_ref, ps_ref, gf_ref, h_ref, u_ref, ext_ref, lv_ref,
                 *, windows, halo, pos0):
    i = pl.program_id(1)
    tm = x_ref.shape[1]
    c = pw_ref.shape[1]
    pad = ext_ref.shape[0] - halo - tm
    top = pad + halo
    n = halo + tm
    x = x_ref[0]
    gm = gm_ref[...]
    before = jnp.where(i == 0, first_ref[...], prev_ref[0])
    ext_ref[0:pad] = jnp.zeros((pad, ext_ref.shape[1]), F32)
    lv_ref[:, 0:pad] = jnp.zeros((2, pad, c), F32)
    ext_ref[pad:top] = _rms(before, RMS_EPS) * gm
    ext_ref[top:] = _rms(x, RMS_EPS) * gm
    pos = pos0 + i * tm + lax.broadcasted_iota(jnp.int32, (tm, 1), 0)
    for g, w in enumerate(windows):
        cs = slice(g * c, (g + 1) * c)
        lv_ref[0, pad:] = ext_ref[pad:, cs] + ext_ref[pad - 1:pad - 1 + n, cs]
        cur, half = 0, 2
        while half < w:
            lv_ref[1 - cur, pad:] = lv_ref[cur, pad:] + lv_ref[cur, pad - half:pad - half + n]
            cur, half = 1 - cur, 2 * half
        inv_cnt = 1.0 / jnp.minimum(pos + 1, w).astype(F32)
        pooled = lv_ref[cur, top:] * inv_cnt - ext_ref[top:, cs]
        out = jnp.dot(pooled.astype(BF16), pw_ref[g], preferred_element_type=F32)
        h_ref[0, :, cs] = x[:, cs] + out * ps_ref[:, cs]
    h = h_ref[0]
    u_ref[0] = (_rms(h, RMS_EPS) * gf_ref[...]).astype(BF16)


def _pool_layer(x, first_rows, g_mix, pool_w, pool_scale, g_ffn, pos0):
    b, l, d = x.shape
    halo = max(POOL_WINDOWS)
    assert first_rows.shape == (halo, d)
    tm = _pick(l, 256, halo)
    nb = tm // halo
    g, c, _ = pool_w.shape
    kern = functools.partial(_pool_kernel, windows=POOL_WINDOWS, halo=halo, pos0=pos0)
    row = lambda bi, i: (bi, i, 0)
    const2 = lambda bi, i: (0, 0)
    blocks = [((1, tm, d), F32), ((1, halo, d), F32), ((halo, d), F32), ((g, c, c), BF16),
              ((1, tm, d), F32), ((1, tm, d), BF16)]
    assert all(w >= 2 and w & (w - 1) == 0 for w in POOL_WINDOWS)
    rows = F32_SUBLANES + halo + tm
    scratch = [((rows, d), F32), ((2, rows, c), F32)]
    return pl.pallas_call(
        kern,
        grid=(b, l // tm),
        in_specs=[
            pl.BlockSpec((1, tm, d), row),
            pl.BlockSpec((1, halo, d), lambda bi, i: (bi, jnp.maximum(i * nb - 1, 0), 0)),
            pl.BlockSpec((halo, d), const2),
            pl.BlockSpec((1, d), const2),
            pl.BlockSpec((g, c, c), lambda bi, i: (0, 0, 0)),
            pl.BlockSpec((1, d), const2),
            pl.BlockSpec((1, d), const2),
        ],
        out_specs=[pl.BlockSpec((1, tm, d), row), pl.BlockSpec((1, tm, d), row)],
        out_shape=[jax.ShapeDtypeStruct((b, l, d), F32), jax.ShapeDtypeStruct((b, l, d), BF16)],
        scratch_shapes=[pltpu.VMEM(s, t) for s, t in scratch],
        compiler_params=_params(("arbitrary", "arbitrary"), blocks, scratch),
        name="pool_layer",
    )(x, x, first_rows, g_mix, pool_w, pool_scale, g_ffn)


def _runs_from_tile_expert(tile_expert, n_used, n_exp):
    te = tile_expert
    prev = jnp.concatenate([te[:1] - 1, te[:-1]])
    start = (te != prev).astype(jnp.int32)
    ids = jnp.arange(n_exp, dtype=jnp.int32)
    present = jnp.any(te[None, :] == ids[:, None], axis=1)
    later = jnp.logical_and(present[None, :], ids[None, :] > ids[:, None])
    next_of = jnp.min(jnp.where(later, ids[None, :], n_exp), axis=1)
    next_of = jnp.where(next_of == n_exp, te[0], next_of).astype(jnp.int32)
    return te, n_used, start, next_of[te], (te == te[-1]).astype(jnp.int32)


def _qkv_epilogue(j, x, wb_ref, ss_ref, qn_ref, kn_ref, o_ref, *, n_q_blocks, hd):
    row_scale = lax.rsqrt(jnp.sum(ss_ref[...], axis=1, keepdims=True) * (1.0 / x.shape[1]) + RMS_EPS)
    is_qk = j < 2 * n_q_blocks
    gain = jnp.where(j < n_q_blocks, qn_ref[...] * hd ** -0.5, kn_ref[...])
    for c in range(o_ref.shape[1] // (2 * hd)):
        acc = jnp.dot(x, wb_ref[0, :, c * 2 * hd:(c + 1) * 2 * hd], preferred_element_type=F32) * row_scale
        for mp in range(2):
            blk = acc[:, mp * hd:(mp + 1) * hd]
            cs = slice((2 * c + mp) * hd, (2 * c + mp + 1) * hd)
            o_ref[:, cs] = jnp.where(is_qk, _rms(blk, RMS_EPS) * gain, blk).astype(o_ref.dtype)


def _staged_kernel(te_ref, nu_ref, st_ref, nx_ref, lr_ref, x_ref, *rest, n_w, n_extra, packed, epilogue):
    w_hbm = rest[:n_w]
    extras = rest[n_w:n_w + n_extra]
    o_ref, stage_ref, wb_ref, sem = rest[n_w + n_extra:]
    j = pl.program_id(0)
    i = pl.program_id(1)
    bn = stage_ref.shape[2]
    used = i < nu_ref[0]

    def copies(e, jj):
        c0 = pl.multiple_of(jj * bn, bn)
        return [pltpu.make_async_copy(w.at[e, :, pl.ds(c0, bn)], stage_ref.at[n], sem.at[n])
                for n, w in enumerate(w_hbm)]

    @pl.when(jnp.logical_and(used, st_ref[i] == 1))
    def _():
        @pl.when(jnp.logical_and(j == 0, i == 0))
        def _():
            for c in copies(te_ref[0], 0):
                c.start()

        for c in copies(te_ref[i], j):
            c.wait()
        wb_ref[...] = stage_ref[...].astype(BF16)
        j_next = j + lr_ref[i]

        @pl.when(j_next < pl.num_programs(0))
        def _():
            for c in copies(nx_ref[i], j_next):
                c.start()

    @pl.when(used)
    def _():
        if packed:
            d2 = x_ref.shape[1]
            lo, hi = _unpack_bf16_pairs(x_ref[...])
            parts = [(lo.astype(BF16), slice(0, d2)), (hi.astype(BF16), slice(d2, 2 * d2))]
        else:
            parts = [(x_ref[...], slice(None))]
        if callable(epilogue):
            epilogue(j, parts[0][0], wb_ref, *extras, o_ref)
            return
        prods = [sum(jnp.dot(xp, wb_ref[n, ks], preferred_element_type=F32) for xp, ks in parts)
                 for n in range(n_w)]
        if epilogue == "swiglu":
            g, u = prods
            o_ref[...] = (g * jax.nn.sigmoid(g) * u).astype(o_ref.dtype)
        else:
            o_ref[...] = _pack_bf16_pairs(prods[0])

    @pl.when(jnp.logical_not(used))
    def _():
        o_ref[...] = jnp.zeros_like(o_ref)


def _staged_matmul(x, weights, runs, tm, bn, packed, name, epilogue, extras=(), extra_specs=(), extra_blocks=()):
    rows = x.shape[0]
    _, k, n = weights[0].shape
    n_w = len(weights)
    kern = functools.partial(_staged_kernel, n_w=n_w, n_extra=len(extras), packed=packed, epilogue=epilogue)
    out_cols, out_bn, out_dtype = (n // 2, bn // 2, jnp.uint32) if epilogue == "pack" else (n, bn, BF16)
    blocks = [((tm, x.shape[1]), x.dtype), ((tm, out_bn), out_dtype), *extra_blocks]
    scratch = [((n_w, k, bn), F32), ((n_w, k, bn), BF16)]
    grid_spec = pltpu.PrefetchScalarGridSpec(
        num_scalar_prefetch=5,
        grid=(n // bn, rows // tm),
        in_specs=[pl.BlockSpec((tm, x.shape[1]), lambda j, i, te, nu, *_: (jnp.minimum(i, nu[0] - 1), 0))]
        + [pl.BlockSpec(memory_space=pl.ANY)] * n_w + list(extra_specs),
        out_specs=pl.BlockSpec((tm, out_bn), lambda j, i, *_: (i, j)),
        scratch_shapes=[pltpu.VMEM(s, t) for s, t in scratch] + [pltpu.SemaphoreType.DMA((n_w,))],
    )
    return pl.pallas_call(
        kern,
        grid_spec=grid_spec,
        out_shape=jax.ShapeDtypeStruct((rows, out_cols), out_dtype),
        compiler_params=_params(("arbitrary", "arbitrary"), blocks, scratch),
        name=name,
    )(*runs, x, *weights, *extras)


def _single_run(m, bm):
    n_tiles = m // bm
    return _runs_from_tile_expert(jnp.zeros((n_tiles,), jnp.int32), jnp.full((1,), n_tiles, jnp.int32), 1)


def _gateup(x, wg, wu):
    m = x.shape[0]
    bm = _pick(m, 1024, BF16_SUBLANES)
    bn = _pick(wg.shape[2], 512, V7X_LANES)
    return _staged_matmul(x, (wg, wu), _single_run(m, bm), bm, bn, False, "ffn_gateup", "swiglu")


def _matmul_residual_kernel(a_ref, w_ref, r_ref, *rest, for_next_norm):
    if for_next_norm:
        g_ref, o_ref, og_ref, ss_ref = rest
    else:
        (o_ref,) = rest
    j = pl.program_id(1)
    kk = pl.program_id(2)
    part = jnp.dot(a_ref[...], w_ref[...], preferred_element_type=F32)

    @pl.when(kk == 0)
    def _():
        o_ref[...] = r_ref[...] + part

    @pl.when(kk != 0)
    def _():
        o_ref[...] += part

    if for_next_norm:
        @pl.when(kk == pl.num_programs(2) - 1)
        def _():
            o = o_ref[...]
            og_ref[...] = (o * g_ref[...]).astype(og_ref.dtype)
            sq = o * o
            lane_tiles = [sq[:, c * V7X_LANES:(c + 1) * V7X_LANES] for c in range(sq.shape[1] // V7X_LANES)]
            partial = functools.reduce(lambda x, y: x + y, lane_tiles)

            @pl.when(j == 0)
            def _():
                ss_ref[...] = partial

            @pl.when(j != 0)
            def _():
                ss_ref[...] += partial


def _matmul_residual(a, w, r, bk_target, next_gain=None):
    m, k = a.shape
    n = w.shape[1]
    for_next_norm = next_gain is not None
    bm = _pick(m, 1024, BF16_SUBLANES)
    bn = _pick(n, 1024, V7X_LANES)
    bk = _pick(k, bk_target, V7X_LANES)
    blocks = [((bm, bk), BF16), ((bk, bn), BF16), ((bm, bn), F32), ((bm, bn), F32)]
    in_specs = [
        pl.BlockSpec((bm, bk), lambda i, j, kk: (i, kk)),
        pl.BlockSpec((bk, bn), lambda i, j, kk: (kk, j)),
        pl.BlockSpec((bm, bn), lambda i, j, kk: (i, j)),
    ]
    out_specs = [pl.BlockSpec((bm, bn), lambda i, j, kk: (i, j))]
    out_shape = [jax.ShapeDtypeStruct((m, n), F32)]
    operands = [a, w, r]
    if for_next_norm:
        blocks += [((bm, bn), BF16), ((bm, V7X_LANES), F32)]
        in_specs.append(pl.BlockSpec((1, bn), lambda i, j, kk: (0, j)))
        out_specs += [pl.BlockSpec((bm, bn), lambda i, j, kk: (i, j)),
                      pl.BlockSpec((bm, V7X_LANES), lambda i, j, kk: (i, 0))]
        out_shape += [jax.ShapeDtypeStruct((m, n), BF16), jax.ShapeDtypeStruct((m, V7X_LANES), F32)]
        operands.append(next_gain)
    out = pl.pallas_call(
        functools.partial(_matmul_residual_kernel, for_next_norm=for_next_norm),
        grid=(m // bm, n // bn, k // bk),
        in_specs=in_specs,
        out_specs=out_specs,
        out_shape=out_shape,
        compiler_params=_params(("arbitrary", "arbitrary", "arbitrary"), blocks),
        name="matmul_residual",
    )(*operands)
    return out if for_next_norm else out[0]


def _qkv_proj(hg, ss, w, q_norm, k_norm):
    m, d = hg.shape
    bm = _pick(m, 1024, BF16_SUBLANES)
    bn = _pick(d, 1024, 2 * HEAD_DIM)
    head = pl.BlockSpec((1, HEAD_DIM), lambda j, i, *_: (0, 0))
    return _staged_matmul(
        hg, (w,), _single_run(m, bm), bm, bn, False, "qkv_proj",
        functools.partial(_qkv_epilogue, n_q_blocks=d // bn, hd=HEAD_DIM),
        extras=(ss, q_norm, k_norm),
        extra_specs=(pl.BlockSpec((bm, V7X_LANES), lambda j, i, *_: (i, 0)), head, head),
        extra_blocks=(((bm, V7X_LANES), F32),),
    )


def _attn_kernel(q_ref, k_ref, v_ref, km_ref, vm_ref, lq1_ref, lk1_ref, lq2_ref, lk2_ref, sub_ref,
                 o_ref, acc_ref, m_ref, l_ref, *, tk, hd, n_meta, lambda_init):
    qi = pl.program_id(2)
    tq = q_ref.shape[0]
    nt = (((1,), (1,)), ((), ()))

    def update(mp, k, v, mask, first):
        q = q_ref[:, mp * hd:(mp + 1) * hd]
        s = lax.dot_general(q, k, nt, preferred_element_type=F32)
        if mask is not None:
            s = jnp.where(mask, s, MASK_VALUE)
        row_max = jnp.max(s, axis=1, keepdims=True)
        if first:
            m_new = row_max
        else:
            m_old = m_ref[mp][:, :1]
            m_new = jnp.maximum(m_old, row_max)
        p = jnp.exp(s - m_new)
        row_sum = jnp.sum(p, axis=1, keepdims=True)
        pv = jnp.dot(p.astype(BF16), v, preferred_element_type=F32)
        if first:
            l_new = row_sum
            acc_ref[mp] = pv
        else:
            alpha = jnp.exp(m_old - m_new)
            l_new = alpha * l_ref[mp][:, :1] + row_sum
            acc_ref[mp] = alpha * acc_ref[mp] + pv
        m_ref[mp] = jnp.broadcast_to(m_new, (tq, V7X_LANES))
        l_ref[mp] = jnp.broadcast_to(l_new, (tq, V7X_LANES))

    meta_mask = lax.broadcasted_iota(jnp.int32, (tq, km_ref.shape[0]), 1) < n_meta
    for mp in range(2):
        update(mp, km_ref[:, mp * hd:(mp + 1) * hd], vm_ref[...], meta_mask, True)

    n_full = (qi * tq) // tk

    def full_tile(kt, carry):
        k0 = pl.multiple_of(kt * tk, tk)
        for mp in range(2):
            update(mp, k_ref[pl.ds(k0, tk), mp * hd:(mp + 1) * hd], v_ref[pl.ds(k0, tk), :], None, False)
        return carry

    lax.fori_loop(0, n_full, full_tile, 0)

    row = lax.broadcasted_iota(jnp.int32, (tq, tk), 0)
    col = lax.broadcasted_iota(jnp.int32, (tq, tk), 1)
    for dt in range(tq // tk):
        k0 = pl.multiple_of(qi * tq + dt * tk, tk)
        diag_mask = col + dt * tk <= row
        for mp in range(2):
            update(mp, k_ref[pl.ds(k0, tk), mp * hd:(mp + 1) * hd], v_ref[pl.ds(k0, tk), :], diag_mask, False)

    lam = (jnp.exp(jnp.sum(lq1_ref[...] * lk1_ref[...], axis=-1, keepdims=True))
           - jnp.exp(jnp.sum(lq2_ref[...] * lk2_ref[...], axis=-1, keepdims=True)) + lambda_init)
    o1 = acc_ref[0] / l_ref[0][:, :1]
    o2 = acc_ref[1] / l_ref[1][:, :1]
    o = o1 - lam * o2
    o = _rms(o, SUBLN_EPS) * sub_ref[...] * (1.0 - lambda_init)
    o_ref[...] = o.astype(o_ref.dtype)


def _attn_bounded_kernel(shift_ref, q_ref, k_ref, v_ref, km_ref, vm_ref, lq1_ref, lk1_ref, lq2_ref, lk2_ref,
                         sub_ref, o_ref, kt_ref, acc_ref, ls_ref, *, hd, n_meta, lambda_init):
    qi = pl.program_id(2)
    tq = q_ref.shape[0]
    n_kt, _, tk = kt_ref.shape
    shift = shift_ref[0, 0]

    @pl.when(qi == 0)
    def _():
        def transpose_tile(c, carry):
            r0 = pl.multiple_of(c * tk, tk)
            kt_ref[c] = k_ref[pl.ds(r0, tk), :].T
            return carry

        lax.fori_loop(0, n_kt, transpose_tile, 0)

    all_rows = slice(0, tq)

    def accumulate(mp, rows, s, v, mask, first):
        p = jnp.exp(s - shift)
        if mask is not None:
            p = jnp.where(mask, p, 0.0)
        part = p[:, 0:V7X_LANES]
        for c in range(1, p.shape[1] // V7X_LANES):
            part = part + p[:, c * V7X_LANES:(c + 1) * V7X_LANES]
        pv = jnp.dot(p.astype(BF16), v, preferred_element_type=F32)
        if first:
            ls_ref[mp, rows] = part
            acc_ref[mp, rows] = pv
        else:
            ls_ref[mp, rows] += part
            acc_ref[mp, rows] += pv

    def kv_tile(kt, rows, mask, first=False):
        k0 = pl.multiple_of(kt * tk, tk)
        v = v_ref[pl.ds(k0, tk), :]
        for mp in range(2):
            hs = slice(mp * hd, (mp + 1) * hd)
            s = jnp.dot(q_ref[rows, hs], kt_ref[kt, hs, :], preferred_element_type=F32)
            accumulate(mp, rows, s, v, mask, first)

    def causal(n_rows):
        row = lax.broadcasted_iota(jnp.int32, (n_rows, tk), 0)
        col = lax.broadcasted_iota(jnp.int32, (n_rows, tk), 1)
        return col <= row

    kv_tile(2 * qi, all_rows, causal(tq), first=True)

    def full_tiles(first, count):
        for t in range(count):
            kv_tile(first + t, all_rows, None)

    def four_full_tiles(quad, carry):
        full_tiles(4 * quad, 4)
        return carry

    lax.fori_loop(0, qi // 2, four_full_tiles, 0)

    @pl.when(qi % 2 == 1)
    def _():
        full_tiles(2 * qi - 2, 2)

    nt = (((1,), (1,)), ((), ()))
    meta_mask = lax.broadcasted_iota(jnp.int32, (tq, km_ref.shape[0]), 1) < n_meta
    for mp in range(2):
        hs = slice(mp * hd, (mp + 1) * hd)
        s = lax.dot_general(q_ref[:, hs], km_ref[:, hs], nt, preferred_element_type=F32)
        accumulate(mp, all_rows, s, vm_ref[...], meta_mask, False)
    kv_tile(2 * qi + 1, slice(tk, tq), causal(tq - tk))

    lam = (jnp.exp(jnp.sum(lq1_ref[...] * lk1_ref[...], axis=-1, keepdims=True))
           - jnp.exp(jnp.sum(lq2_ref[...] * lk2_ref[...], axis=-1, keepdims=True)) + lambda_init)
    o1 = acc_ref[0] / jnp.sum(ls_ref[0], axis=1, keepdims=True)
    o2 = acc_ref[1] / jnp.sum(ls_ref[1], axis=1, keepdims=True)
    o = o1 - lam * o2
    o = _rms(o, SUBLN_EPS) * sub_ref[...] * (1.0 - lambda_init)
    o_ref[...] = o.astype(o_ref.dtype)


SAFE_LOGIT_BOUND = 32.0


def _diff_attention(qkv, qkv_meta, logit_bound, lq1, lk1, lq2, lk2, subln, batch, n_meta, lambda_init):
    t, n3 = qkv.shape
    d = n3 // 3
    l = t // batch
    hw = 2 * HEAD_DIM
    heads = d // hw
    tq = _pick(l, 1024, 2 * V7X_LANES)
    tk = tq // 2
    nq = l // tq
    mrows = qkv_meta.shape[0]
    vec = pl.BlockSpec((1, HEAD_DIM), lambda b, h, i: (0, 0))
    blocks = [((tq, hw), BF16), ((l, hw), BF16), ((l, hw), BF16), ((mrows, hw), BF16), ((mrows, hw), BF16),
              ((tq, hw), BF16)]
    in_specs = [
        pl.BlockSpec((tq, hw), lambda b, h, i: (b * nq + i, h)),
        pl.BlockSpec((l, hw), lambda b, h, i: (b, heads + h)),
        pl.BlockSpec((l, hw), lambda b, h, i: (b, 2 * heads + h)),
        pl.BlockSpec((mrows, hw), lambda b, h, i: (0, heads + h)),
        pl.BlockSpec((mrows, hw), lambda b, h, i: (0, 2 * heads + h)),
        vec, vec, vec, vec,
        pl.BlockSpec((1, hw), lambda b, h, i: (0, 0)),
    ]
    common = dict(
        grid=(batch, heads, nq),
        out_specs=pl.BlockSpec((tq, hw), lambda b, h, i: (b * nq + i, h)),
        out_shape=jax.ShapeDtypeStruct((t, d), BF16),
    )
    operands = (qkv, qkv, qkv, qkv_meta, qkv_meta, lq1, lk1, lq2, lk2, subln)

    def running_max(_):
        kern = functools.partial(_attn_kernel, tk=tk, hd=HEAD_DIM, n_meta=n_meta, lambda_init=lambda_init)
        scratch = [((2, tq, hw), F32), ((2, tq, V7X_LANES), F32), ((2, tq, V7X_LANES), F32)]
        return pl.pallas_call(
            kern, in_specs=in_specs,
            scratch_shapes=[pltpu.VMEM(s, ty) for s, ty in scratch],
            compiler_params=_params(("arbitrary", "arbitrary", "arbitrary"), blocks, scratch),
            name="diff_attention_running_max", **common,
        )(*operands)

    def bounded(shift):
        kern = functools.partial(_attn_bounded_kernel, hd=HEAD_DIM, n_meta=n_meta, lambda_init=lambda_init)
        scratch = [((l // tk, hw, tk), BF16), ((2, tq, hw), F32), ((2, tq, V7X_LANES), F32)]
        return pl.pallas_call(
            kern,
            in_specs=[pl.BlockSpec((1, 1), lambda b, h, i: (0, 0), memory_space=pltpu.SMEM)] + in_specs,
            scratch_shapes=[pltpu.VMEM(s, ty) for s, ty in scratch],
            compiler_params=_params(("arbitrary", "arbitrary", "arbitrary"), blocks, scratch),
            name="diff_attention_bounded", **common,
        )(shift, *operands)

    shift = logit_bound.reshape(1, 1).astype(F32)
    return lax.cond(logit_bound <= SAFE_LOGIT_BOUND, bounded, running_max, shift)


def _router_kernel(x_ref, g_ref, r_ref, up_ref, rt_ref, *, n_exp):
    x = x_ref[...]
    u = _rms(x, RMS_EPS) * g_ref[...]
    u_hi = u.astype(BF16)
    u_lo = (u - u_hi.astype(F32)).astype(BF16)
    both = (jnp.dot(u_hi, r_ref[...], preferred_element_type=F32)
            + jnp.dot(u_lo, r_ref[...], preferred_element_type=F32))
    logits = both + pltpu.roll(both, V7X_LANES - n_exp, 1)
    lane = lax.broadcasted_iota(jnp.int32, logits.shape, 1).astype(F32)
    lg = jnp.where(lane < n_exp, logits, -jnp.inf)
    m1 = jnp.max(lg, axis=1, keepdims=True)
    i1 = jnp.min(jnp.where(lg == m1, lane, float(V7X_LANES)), axis=1, keepdims=True)
    lg2 = jnp.where(lane == i1, -jnp.inf, lg)
    m2 = jnp.max(lg2, axis=1, keepdims=True)
    i2 = jnp.min(jnp.where(lg2 == m2, lane, float(V7X_LANES)), axis=1, keepdims=True)
    e = jnp.exp(m2 - m1)
    g1 = 1.0 / (1.0 + e)
    g2 = e / (1.0 + e)
    rt = jnp.where(lane == 0, i1, jnp.where(lane == 1, i2, jnp.where(lane == 2, g1, jnp.where(lane == 3, g2, 0.0))))
    rt_ref[...] = rt
    up_ref[...] = _pack_bf16_pairs(u)


def _router(h, g, router_split, n_exp):
    m, d = h.shape
    bm = _pick(m, 256, 8)
    kern = functools.partial(_router_kernel, n_exp=n_exp)
    blocks = [((bm, d), F32), ((d, V7X_LANES), BF16), ((bm, d // 2), jnp.uint32), ((bm, V7X_LANES), F32)]
    return pl.pallas_call(
        kern,
        grid=(m // bm,),
        in_specs=[
            pl.BlockSpec((bm, d), lambda i: (i, 0)),
            pl.BlockSpec((1, d), lambda i: (0, 0)),
            pl.BlockSpec((d, V7X_LANES), lambda i: (0, 0)),
        ],
        out_specs=[pl.BlockSpec((bm, d // 2), lambda i: (i, 0)), pl.BlockSpec((bm, V7X_LANES), lambda i: (i, 0))],
        out_shape=[jax.ShapeDtypeStruct((m, d // 2), jnp.uint32), jax.ShapeDtypeStruct((m, V7X_LANES), F32)],
        compiler_params=_params(("arbitrary",), blocks),
        name="router",
    )(h, g, router_split)


ROW_DMA_UNROLL = 8


def _dispatch_kernel(pos_ref, u_ref, xs_in_ref, xs_ref, sem):
    del xs_in_ref
    bm = u_ref.shape[0]

    def row_copy(r, s):
        return pltpu.make_async_copy(u_ref.at[pl.ds(r, 1)], xs_ref.at[pl.ds(pos_ref[0, 0, 2 * r + s], 1)], sem)

    def start(r, c):
        row_copy(r, 0).start(priority=0)
        row_copy(r, 1).start(priority=1)
        return c

    def wait(r, c):
        row_copy(r, 0).wait()
        row_copy(r, 1).wait()
        return c

    lax.fori_loop(0, bm, start, 0, unroll=ROW_DMA_UNROLL)
    lax.fori_loop(0, bm, wait, 0, unroll=ROW_DMA_UNROLL)


def _dispatch(up, pos_tiles, xs_zero):
    m, d2 = up.shape
    bm = pos_tiles.shape[2] // TOP_K
    blocks = [((bm, d2), jnp.uint32)]
    return pl.pallas_call(
        _dispatch_kernel,
        grid=(m // bm,),
        in_specs=[
            pl.BlockSpec((1, 1, TOP_K * bm), lambda i: (i, 0, 0), memory_space=pltpu.SMEM),
            pl.BlockSpec((bm, d2), lambda i: (i, 0)),
            pl.BlockSpec(memory_space=pl.ANY),
        ],
        out_specs=pl.BlockSpec(memory_space=pl.ANY),
        out_shape=jax.ShapeDtypeStruct(xs_zero.shape, xs_zero.dtype),
        scratch_shapes=[pltpu.SemaphoreType.DMA(())],
        input_output_aliases={2: 0},
        compiler_params=_params(("arbitrary",), blocks),
        name="moe_dispatch",
    )(pos_tiles, up, xs_zero)


def _combine_kernel(pos_ref, pos_next_ref, h_ref, rt_ref, y_ref, o_ref, buf_ref, sem, *, y_block):
    i = pl.program_id(0)
    bm = h_ref.shape[0]
    slot = i % 2

    def row_copy(p_ref, sl, r, s):
        return pltpu.make_async_copy(y_ref.at[pl.ds(p_ref[0, 0, 2 * r + s], 1)],
                                     buf_ref.at[sl, s, pl.ds(r, 1)], sem.at[sl])

    def start(p_ref, sl, r):
        row_copy(p_ref, sl, r, 0).start(priority=0)
        row_copy(p_ref, sl, r, 1).start(priority=1)

    def wait_all(p_ref, sl):
        def wait(r, c):
            row_copy(p_ref, sl, r, 0).wait()
            row_copy(p_ref, sl, r, 1).wait()
            return c

        lax.fori_loop(0, bm, wait, 0, unroll=ROW_DMA_UNROLL)

    @pl.when(i == 0)
    def _():
        lax.fori_loop(0, bm, lambda r, c: (start(pos_ref, 0, r), c)[1], 0, unroll=ROW_DMA_UNROLL)

    wait_all(pos_ref, slot)
    for r in range(bm):
        start(pos_next_ref, 1 - slot, r)
    rt = rt_ref[...]
    g1, g2 = rt[:, 2:3], rt[:, 3:4]
    half = y_block // 2
    for blk in range(h_ref.shape[1] // y_block):
        words = slice(blk * half, (blk + 1) * half)
        lo1, hi1 = _unpack_bf16_pairs(buf_ref[slot, 0, :, words])
        lo2, hi2 = _unpack_bf16_pairs(buf_ref[slot, 1, :, words])
        c_lo = slice(blk * y_block, blk * y_block + half)
        c_hi = slice(blk * y_block + half, (blk + 1) * y_block)
        o_ref[:, c_lo] = h_ref[:, c_lo] + g1 * lo1 + g2 * lo2
        o_ref[:, c_hi] = h_ref[:, c_hi] + g1 * hi1 + g2 * hi2

    @pl.when(i == pl.num_programs(0) - 1)
    def _():
        wait_all(pos_next_ref, 1 - slot)


def _combine(h, route, y, y_block, pos_tiles):
    m, d = h.shape
    n_tiles = pos_tiles.shape[0]
    bm = pos_tiles.shape[2] // TOP_K
    blocks = [((bm, d), F32), ((bm, V7X_LANES), F32), ((bm, d), F32)]
    scratch = [((2, TOP_K, bm, d // 2), jnp.uint32)]
    pos_spec = lambda index: pl.BlockSpec((1, 1, TOP_K * bm), index, memory_space=pltpu.SMEM)
    return pl.pallas_call(
        functools.partial(_combine_kernel, y_block=y_block),
        grid=(n_tiles,),
        in_specs=[
            pos_spec(lambda i: (i, 0, 0)),
            pos_spec(lambda i: (jnp.minimum(i + 1, n_tiles - 1), 0, 0)),
            pl.BlockSpec((bm, d), lambda i: (i, 0)),
            pl.BlockSpec((bm, V7X_LANES), lambda i: (i, 0)),
            pl.BlockSpec(memory_space=pl.ANY),
        ],
        out_specs=pl.BlockSpec((bm, d), lambda i: (i, 0)),
        out_shape=jax.ShapeDtypeStruct((m, d), F32),
        scratch_shapes=[pltpu.VMEM(*scratch[0]), pltpu.SemaphoreType.DMA((2,))],
        compiler_params=_params(("arbitrary",), blocks, scratch),
        name="moe_combine",
    )(pos_tiles, pos_tiles, h, route, y)


def _routing_tables(route, n_exp, tm, n_tiles):
    experts = route[:, :TOP_K].astype(jnp.int32).reshape(-1)
    onehot = (experts[:, None] == jnp.arange(n_exp, dtype=jnp.int32)[None, :]).astype(jnp.int32)
    csum = jnp.cumsum(onehot, axis=0)
    rank = jnp.sum(csum * onehot, axis=1) - 1
    counts = csum[-1]
    padded = ((counts + tm - 1) // tm) * tm
    ends = jnp.cumsum(padded)
    starts = ends - padded
    pos = jnp.sum(starts[None, :] * onehot, axis=1) + rank
    n_used = (ends[-1] // tm).astype(jnp.int32)
    tile_row = jnp.minimum(jnp.arange(n_tiles, dtype=jnp.int32), n_used - 1) * tm
    tile_expert = jnp.sum((tile_row[:, None] >= ends[None, :]).astype(jnp.int32), axis=1)
    return pos, jnp.minimum(tile_expert, n_exp - 1).astype(jnp.int32), n_used.reshape(1)


def kernel(x, meta_tokens, norm_mix, norm_ffn, pool_w, pool_scale, ffn_w_gate, ffn_w_up, ffn_w_down,
           w_qkv, q_norm, k_norm, lambda_q1, lambda_k1, lambda_q2, lambda_k2, subln, w_o,
           router, exp_w_gate, exp_w_up, exp_w_down):
    batch, seq, d = x.shape
    n_meta = meta_tokens.shape[0]
    n_exp = router.shape[-1]
    assert norm_mix.shape[0] == 2, "one pooling layer followed by one attention layer"
    halo = max(POOL_WINDOWS)
    assert n_meta >= halo and n_meta <= V7X_LANES
    t = batch * seq
    lambda_init = 0.8 - 0.6 * math.exp(-0.3 * 1)

    row = lambda v: v.reshape(1, -1).astype(F32)
    pool_wb = pool_w[0].astype(BF16)
    wg, wu, wqkv = ffn_w_gate.astype(F32), ffn_w_up.astype(F32), w_qkv.astype(F32)
    wd, wo = ffn_w_down[0].astype(BF16), w_o[0].astype(BF16)
    ewg, ewu, ewd = (w.reshape(w.shape[1:]).astype(F32) for w in (exp_w_gate, exp_w_up, exp_w_down))
    r_hi = router[0].astype(BF16)
    r_lo = (router[0].astype(F32) - r_hi.astype(F32)).astype(BF16)
    router_split = jnp.pad(jnp.concatenate([r_hi, r_lo], axis=1), ((0, 0), (0, V7X_LANES - 2 * n_exp)))

    def layer0_and_qkv(tokens, first_rows, pos0):
        b, l, _ = tokens.shape
        h1, u2 = _pool_layer(tokens, first_rows, row(norm_mix[0]), pool_wb, row(pool_scale[0]),
                             row(norm_ffn[0]), pos0)
        a = _gateup(u2.reshape(b * l, d), wg, wu)
        h2, h2g, h2_ss = _matmul_residual(a, wd, h1.reshape(b * l, d), 3584, next_gain=row(norm_mix[1]))
        qkv = _qkv_proj(h2g, h2_ss, wqkv, row(q_norm[0]), row(k_norm[0]))
        return h2, qkv

    meta = meta_tokens.astype(F32)
    _, qkv_meta = layer0_and_qkv(meta[None], jnp.zeros((halo, d), F32), 0)
    qkv_meta = jnp.pad(qkv_meta, ((0, V7X_LANES - n_meta), (0, 0)))
    h2, qkv = layer0_and_qkv(x, meta[n_meta - halo:], n_meta)

    logit_bound = 1.01 * math.sqrt(HEAD_DIM) * jnp.max(jnp.abs(q_norm[0])) * jnp.max(jnp.abs(k_norm[0]))
    o = _diff_attention(qkv, qkv_meta, logit_bound.astype(F32), row(lambda_q1[0]), row(lambda_k1[0]),
                        row(lambda_q2[0]), row(lambda_k2[0]), row(subln[0]), batch, n_meta, lambda_init)
    h3 = _matmul_residual(o, wo, h2, d)

    up, route = _router(h3, row(norm_ffn[1]), router_split, n_exp)
    tm = _pick(t, 512, BF16_SUBLANES)
    n_tiles = (TOP_K * t) // tm + n_exp
    pos, tile_expert, n_used = _routing_tables(route, n_exp, tm, n_tiles)
    bm_rows = _pick(t, 256, 8)
    pos_tiles = pos.reshape(t // bm_rows, 1, TOP_K * bm_rows)
    xs = _dispatch(up, pos_tiles, jnp.zeros((n_tiles * tm, d // 2), jnp.uint32))
    runs = _runs_from_tile_expert(tile_expert, n_used, n_exp)
    a = _staged_matmul(xs, (ewg, ewu), runs, tm, _pick(ewg.shape[2], 512, V7X_LANES), True, "expert_gateup", "swiglu")
    y_block = _pick(d, 1024, 2 * V7X_LANES)
    y = _staged_matmul(a, (ewd,), runs, tm, y_block, False, "expert_down", "pack")
    out = _combine(h3, route, y, y_block, pos_tiles)
    return out.reshape(batch, seq, d)
```

```python
import functools
import math

import jax
import jax.numpy as jnp
from jax import lax
from jax.experimental import pallas as pl
from jax.experimental.pallas import tpu as pltpu

POOL_WINDOWS = (2, 4, 8, 16)
HEAD_DIM = 128
TOP_K = 2
RMS_EPS = 1e-6
SUBLN_EPS = 1e-5
MASK_VALUE = -1e30

V7X_LANES = 128
V7X_VMEM_BYTES = 64 * 1024 * 1024
V7X_VMEM_RESERVE_BYTES = 6 * 1024 * 1024
VMEM_TEMPORARIES_BYTES = 16 * 1024 * 1024
BF16_SUBLANES = 16
F32_SUBLANES = 8

F32 = jnp.float32
BF16 = jnp.bfloat16


def _pick(dim, target, mult):
    for b in range(min(dim, target), 0, -1):
        if dim % b == 0 and b % mult == 0:
            return b
    return dim


def _nbytes(shape, dtype):
    return math.prod(shape) * jnp.dtype(dtype).itemsize


def _params(semantics, pipelined, scratch=()):
    est = 2 * sum(_nbytes(s, d) for s, d in pipelined) + sum(_nbytes(s, d) for s, d in scratch)
    limit = min(V7X_VMEM_BYTES - V7X_VMEM_RESERVE_BYTES, est + VMEM_TEMPORARIES_BYTES)
    return pltpu.CompilerParams(dimension_semantics=semantics, vmem_limit_bytes=limit)


def _rms(v, eps):
    return v * lax.rsqrt(jnp.mean(v * v, axis=-1, keepdims=True) + eps)


def _pack_bf16_pairs(v):
    half = v.shape[1] // 2
    bits = lax.bitcast_convert_type(v.astype(BF16).astype(F32), jnp.uint32)
    return (bits[:, :half] >> 16) | (bits[:, half:] & jnp.uint32(0xFFFF0000))


def _unpack_bf16_pairs(w):
    return (lax.bitcast_convert_type(w << 16, F32),
            lax.bitcast_convert_type(w & jnp.uint32(0xFFFF0000), F32))


def _pool_kernel(x_ref, prev_ref, first_ref, gm_ref, pw_ref, ps_ref, gf_ref, h_ref, u_ref, ext_ref, lv_ref,
                 *, windows, halo, pos0):
    i = pl.program_id(1)
    tm = x_ref.shape[1]
    c = pw_ref.shape[1]
    pad = ext_ref.shape[0] - halo - tm
    top = pad + halo
    n = halo + tm
    x = x_ref[0]
    gm = gm_ref[...]
    before = jnp.where(i == 0, first_ref[...], prev_ref[0])
    ext_ref[0:pad] = jnp.zeros((pad, ext_ref.shape[1]), F32)
    lv_ref[:, 0:pad] = jnp.zeros((2, pad, c), F32)
    ext_ref[pad:top] = _rms(before, RMS_EPS) * gm
    ext_ref[top:] = _rms(x, RMS_EPS) * gm
    pos = pos0 + i * tm + lax.broadcasted_iota(jnp.int32, (tm, 1), 0)
    for g, w in enumerate(windows):
        cs = slice(g * c, (g + 1) * c)
        lv_ref[0, pad:] = ext_ref[pad:, cs] + ext_ref[pad - 1:pad - 1 + n, cs]
        cur, half = 0, 2
        while half < w:
            lv_ref[1 - cur, pad:] = lv_ref[cur, pad:] + lv_ref[cur, pad - half:pad - half + n]
            cur, half = 1 - cur, 2 * half
        inv_cnt = 1.0 / jnp.minimum(pos + 1, w).astype(F32)
        pooled = lv_ref[cur, top:] * inv_cnt - ext_ref[top:, cs]
        out = jnp.dot(pooled.astype(BF16), pw_ref[g], preferred_element_type=F32)
        h_ref[0, :, cs] = x[:, cs] + out * ps_ref[:, cs]
    h = h_ref[0]
    u_ref[0] = (_rms(h, RMS_EPS) * gf_ref[...]).astype(BF16)


def _pool_layer(x, first_rows, g_mix, pool_w, pool_scale, g_ffn, pos0):
    b, l, d = x.shape
    halo = max(POOL_WINDOWS)
    assert first_rows.shape == (halo, d)
    tm = _pick(l, 256, halo)
    nb = tm // halo
    g, c, _ = pool_w.shape
    kern = functools.partial(_pool_kernel, windows=POOL_WINDOWS, halo=halo, pos0=pos0)
    row = lambda bi, i: (bi, i, 0)
    const2 = lambda bi, i: (0, 0)
    blocks = [((1, tm, d), F32), ((1, halo, d), F32), ((halo, d), F32), ((g, c, c), BF16),
              ((1, tm, d), F32), ((1, tm, d), BF16)]
    assert all(w >= 2 and w & (w - 1) == 0 for w in POOL_WINDOWS)
    rows = F32_SUBLANES + halo + tm
    scratch = [((rows, d), F32), ((2, rows, c), F32)]
    return pl.pallas_call(
        kern,
        grid=(b, l // tm),
        in_specs=[
            pl.BlockSpec((1, tm, d), row),
            pl.BlockSpec((1, halo, d), lambda bi, i: (bi, jnp.maximum(i * nb - 1, 0), 0)),
            pl.BlockSpec((halo, d), const2),
            pl.BlockSpec((1, d), const2),
            pl.BlockSpec((g, c, c), lambda bi, i: (0, 0, 0)),
            pl.BlockSpec((1, d), const2),
            pl.BlockSpec((1, d), const2),
        ],
        out_specs=[pl.BlockSpec((1, tm, d), row), pl.BlockSpec((1, tm, d), row)],
        out_shape=[jax.ShapeDtypeStruct((b, l, d), F32), jax.ShapeDtypeStruct((b, l, d), BF16)],
        scratch_shapes=[pltpu.VMEM(s, t) for s, t in scratch],
        compiler_params=_params(("arbitrary", "arbitrary"), blocks, scratch),
        name="pool_layer",
    )(x, x, first_rows, g_mix, pool_w, pool_scale, g_ffn)


def _runs_from_tile_expert(tile_expert, n_used, n_exp):
    te = tile_expert
    prev = jnp.concatenate([te[:1] - 1, te[:-1]])
    start = (te != prev).astype(jnp.int32)
    ids = jnp.arange(n_exp, dtype=jnp.int32)
    present = jnp.any(te[None, :] == ids[:, None], axis=1)
    later = jnp.logical_and(present[None, :], ids[None, :] > ids[:, None])
    next_of = jnp.min(jnp.where(later, ids[None, :], n_exp), axis=1)
    next_of = jnp.where(next_of == n_exp, te[0], next_of).astype(jnp.int32)
    return te, n_used, start, next_of[te], (te == te[-1]).astype(jnp.int32)


def _qkv_epilogue(j, x, wb_ref, ss_ref, qn_ref, kn_ref, o_ref, *, n_q_blocks, hd):
    row_scale = lax.rsqrt(jnp.sum(ss_ref[...], axis=1, keepdims=True) * (1.0 / x.shape[1]) + RMS_EPS)
    is_qk = j < 2 * n_q_blocks
    gain = jnp.where(j < n_q_blocks, qn_ref[...] * hd ** -0.5, kn_ref[...])
    for c in range(o_ref.shape[1] // (2 * hd)):
        acc = jnp.dot(x, wb_ref[0, :, c * 2 * hd:(c + 1) * 2 * hd], preferred_element_type=F32) * row_scale
        for mp in range(2):
            blk = acc[:, mp * hd:(mp + 1) * hd]
            cs = slice((2 * c + mp) * hd, (2 * c + mp + 1) * hd)
            o_ref[:, cs] = jnp.where(is_qk, _rms(blk, RMS_EPS) * gain, blk).astype(o_ref.dtype)


def _staged_kernel(te_ref, nu_ref, st_ref, nx_ref, lr_ref, x_ref, *rest, n_w, n_extra, packed, epilogue):
    w_hbm = rest[:n_w]
    extras = rest[n_w:n_w + n_extra]
    o_ref, stage_ref, wb_ref, sem = rest[n_w + n_extra:]
    j = pl.program_id(0)
    i = pl.program_id(1)
    n_stage, _, _, bn = stage_ref.shape
    used = i < nu_ref[0]

    def copies(e, jj, slot):
        c0 = pl.multiple_of(jj * bn, bn)
        return [pltpu.make_async_copy(w.at[e, :, pl.ds(c0, bn)], stage_ref.at[slot, n], sem.at[slot, n])
                for n, w in enumerate(w_hbm)]

    @pl.when(jnp.logical_and(used, st_ref[i] == 1))
    def _():
        slot = j % n_stage
        j_next = j + lr_ref[i]

        def request_next():
            @pl.when(j_next < pl.num_programs(0))
            def _():
                for c in copies(nx_ref[i], j_next, j_next % n_stage):
                    c.start()

        @pl.when(jnp.logical_and(j == 0, i == 0))
        def _():
            for c in copies(te_ref[0], 0, 0):
                c.start()

        if n_stage == 2:
            request_next()
        for c in copies(te_ref[i], j, slot):
            c.wait()
        wb_ref[...] = stage_ref[slot].astype(BF16)
        if n_stage == 1:
            request_next()

    @pl.when(used)
    def _():
        if packed:
            d2 = x_ref.shape[1]
            lo, hi = _unpack_bf16_pairs(x_ref[...])
            parts = [(lo.astype(BF16), slice(0, d2)), (hi.astype(BF16), slice(d2, 2 * d2))]
        else:
            parts = [(x_ref[...], slice(None))]
        if callable(epilogue):
            epilogue(j, parts[0][0], wb_ref, *extras, o_ref)
            return
        prods = [sum(jnp.dot(xp, wb_ref[n, ks], preferred_element_type=F32) for xp, ks in parts)
                 for n in range(n_w)]
        if epilogue == "swiglu":
            g, u = prods
            o_ref[...] = (g * jax.nn.sigmoid(g) * u).astype(o_ref.dtype)
        else:
            o_ref[...] = _pack_bf16_pairs(prods[0])

    @pl.when(jnp.logical_not(used))
    def _():
        o_ref[...] = jnp.zeros_like(o_ref)


def _staged_matmul(x, weights, runs, tm, bn, packed, name, epilogue, extras=(), extra_specs=(), extra_blocks=(),
                   single_run=False):
    rows = x.shape[0]
    _, k, n = weights[0].shape
    n_w = len(weights)
    kern = functools.partial(_staged_kernel, n_w=n_w, n_extra=len(extras), packed=packed, epilogue=epilogue)
    out_cols, out_bn, out_dtype = (n // 2, bn // 2, jnp.uint32) if epilogue == "pack" else (n, bn, BF16)
    blocks = [((tm, x.shape[1]), x.dtype), ((tm, out_bn), out_dtype), *extra_blocks]
    stage, cast = ((n_w, k, bn), F32), ((n_w, k, bn), BF16)
    fits_two = (2 * sum(_nbytes(*b) for b in blocks) + 2 * _nbytes(*stage) + _nbytes(*cast)
                + VMEM_TEMPORARIES_BYTES <= V7X_VMEM_BYTES - V7X_VMEM_RESERVE_BYTES)
    n_stage = 2 if (single_run and fits_two) else 1
    scratch = [((n_stage, n_w, k, bn), F32), cast]
    grid_spec = pltpu.PrefetchScalarGridSpec(
        num_scalar_prefetch=5,
        grid=(n // bn, rows // tm),
        in_specs=[pl.BlockSpec((tm, x.shape[1]), lambda j, i, te, nu, *_: (jnp.minimum(i, nu[0] - 1), 0))]
        + [pl.BlockSpec(memory_space=pl.ANY)] * n_w + list(extra_specs),
        out_specs=pl.BlockSpec((tm, out_bn), lambda j, i, *_: (i, j)),
        scratch_shapes=[pltpu.VMEM(s, t) for s, t in scratch] + [pltpu.SemaphoreType.DMA((n_stage, n_w))],
    )
    return pl.pallas_call(
        kern,
        grid_spec=grid_spec,
        out_shape=jax.ShapeDtypeStruct((rows, out_cols), out_dtype),
        compiler_params=_params(("arbitrary", "arbitrary"), blocks, scratch),
        name=name,
    )(*runs, x, *weights, *extras)


def _single_run(m, bm):
    n_tiles = m // bm
    return _runs_from_tile_expert(jnp.zeros((n_tiles,), jnp.int32), jnp.full((1,), n_tiles, jnp.int32), 1)


def _gateup(x, wg, wu):
    m = x.shape[0]
    bm = _pick(m, 1024, BF16_SUBLANES)
    bn = _pick(wg.shape[2], 512, V7X_LANES)
    return _staged_matmul(x, (wg, wu), _single_run(m, bm), bm, bn, False, "ffn_gateup", "swiglu", single_run=True)


def _matmul_residual_kernel(a_ref, w_ref, r_ref, *rest, for_next_norm):
    if for_next_norm:
        g_ref, o_ref, og_ref, ss_ref = rest
    else:
        (o_ref,) = rest
    j = pl.program_id(1)
    kk = pl.program_id(2)
    part = jnp.dot(a_ref[...], w_ref[...], preferred_element_type=F32)

    @pl.when(kk == 0)
    def _():
        o_ref[...] = r_ref[...] + part

    @pl.when(kk != 0)
    def _():
        o_ref[...] += part

    if for_next_norm:
        @pl.when(kk == pl.num_programs(2) - 1)
        def _():
            o = o_ref[...]
            og_ref[...] = (o * g_ref[...]).astype(og_ref.dtype)
            sq = o * o
            lane_tiles = [sq[:, c * V7X_LANES:(c + 1) * V7X_LANES] for c in range(sq.shape[1] // V7X_LANES)]
            partial = functools.reduce(lambda x, y: x + y, lane_tiles)

            @pl.when(j == 0)
            def _():
                ss_ref[...] = partial

            @pl.when(j != 0)
            def _():
                ss_ref[...] += partial


def _matmul_residual(a, w, r, bk_target, next_gain=None):
    m, k = a.shape
    n = w.shape[1]
    for_next_norm = next_gain is not None
    bm = _pick(m, 1024, BF16_SUBLANES)
    bn = _pick(n, 1024, V7X_LANES)
    bk = _pick(k, bk_target, V7X_LANES)
    blocks = [((bm, bk), BF16), ((bk, bn), BF16), ((bm, bn), F32), ((bm, bn), F32)]
    in_specs = [
        pl.BlockSpec((bm, bk), lambda i, j, kk: (i, kk)),
        pl.BlockSpec((bk, bn), lambda i, j, kk: (kk, j)),
        pl.BlockSpec((bm, bn), lambda i, j, kk: (i, j)),
    ]
    out_specs = [pl.BlockSpec((bm, bn), lambda i, j, kk: (i, j))]
    out_shape = [jax.ShapeDtypeStruct((m, n), F32)]
    operands = [a, w, r]
    if for_next_norm:
        blocks += [((bm, bn), BF16), ((bm, V7X_LANES), F32)]
        in_specs.append(pl.BlockSpec((1, bn), lambda i, j, kk: (0, j)))
        out_specs += [pl.BlockSpec((bm, bn), lambda i, j, kk: (i, j)),
                      pl.BlockSpec((bm, V7X_LANES), lambda i, j, kk: (i, 0))]
        out_shape += [jax.ShapeDtypeStruct((m, n), BF16), jax.ShapeDtypeStruct((m, V7X_LANES), F32)]
        operands.append(next_gain)
    out = pl.pallas_call(
        functools.partial(_matmul_residual_kernel, for_next_norm=for_next_norm),
        grid=(m // bm, n // bn, k // bk),
        in_specs=in_specs,
        out_specs=out_specs,
        out_shape=out_shape,
        compiler_params=_params(("arbitrary", "arbitrary", "arbitrary"), blocks),
        name="matmul_residual",
    )(*operands)
    return out if for_next_norm else out[0]


def _qkv_proj(hg, ss, w, q_norm, k_norm):
    m, d = hg.shape
    bm = _pick(m, 1024, BF16_SUBLANES)
    bn = _pick(d, 1024, 2 * HEAD_DIM)
    head = pl.BlockSpec((1, HEAD_DIM), lambda j, i, *_: (0, 0))
    return _staged_matmul(
        hg, (w,), _single_run(m, bm), bm, bn, False, "qkv_proj",
        functools.partial(_qkv_epilogue, n_q_blocks=d // bn, hd=HEAD_DIM),
        extras=(ss, q_norm, k_norm),
        extra_specs=(pl.BlockSpec((bm, V7X_LANES), lambda j, i, *_: (i, 0)), head, head),
        extra_blocks=(((bm, V7X_LANES), F32),),
        single_run=True,
    )


def _attn_kernel(q_ref, k_ref, v_ref, km_ref, vm_ref, lq1_ref, lk1_ref, lq2_ref, lk2_ref, sub_ref,
                 o_ref, acc_ref, m_ref, l_ref, *, tk, hd, n_meta, lambda_init):
    qi = pl.program_id(2)
    tq = q_ref.shape[0]
    nt = (((1,), (1,)), ((), ()))

    def update(mp, k, v, mask, first):
        q = q_ref[:, mp * hd:(mp + 1) * hd]
        s = lax.dot_general(q, k, nt, preferred_element_type=F32)
        if mask is not None:
            s = jnp.where(mask, s, MASK_VALUE)
        row_max = jnp.max(s, axis=1, keepdims=True)
        if first:
            m_new = row_max
        else:
            m_old = m_ref[mp][:, :1]
            m_new = jnp.maximum(m_old, row_max)
        p = jnp.exp(s - m_new)
        row_sum = jnp.sum(p, axis=1, keepdims=True)
        pv = jnp.dot(p.astype(BF16), v, preferred_element_type=F32)
        if first:
            l_new = row_sum
            acc_ref[mp] = pv
        else:
            alpha = jnp.exp(m_old - m_new)
            l_new = alpha * l_ref[mp][:, :1] + row_sum
            acc_ref[mp] = alpha * acc_ref[mp] + pv
        m_ref[mp] = jnp.broadcast_to(m_new, (tq, V7X_LANES))
        l_ref[mp] = jnp.broadcast_to(l_new, (tq, V7X_LANES))

    meta_mask = lax.broadcasted_iota(jnp.int32, (tq, km_ref.shape[0]), 1) < n_meta
    for mp in range(2):
        update(mp, km_ref[:, mp * hd:(mp + 1) * hd], vm_ref[...], meta_mask, True)

    n_full = (qi * tq) // tk

    def full_tile(kt, carry):
        k0 = pl.multiple_of(kt * tk, tk)
        for mp in range(2):
            update(mp, k_ref[pl.ds(k0, tk), mp * hd:(mp + 1) * hd], v_ref[pl.ds(k0, tk), :], None, False)
        return carry

    lax.fori_loop(0, n_full, full_tile, 0)

    row = lax.broadcasted_iota(jnp.int32, (tq, tk), 0)
    col = lax.broadcasted_iota(jnp.int32, (tq, tk), 1)
    for dt in range(tq // tk):
        k0 = pl.multiple_of(qi * tq + dt * tk, tk)
        diag_mask = col + dt * tk <= row
        for mp in range(2):
            update(mp, k_ref[pl.ds(k0, tk), mp * hd:(mp + 1) * hd], v_ref[pl.ds(k0, tk), :], diag_mask, False)

    lam = (jnp.exp(jnp.sum(lq1_ref[...] * lk1_ref[...], axis=-1, keepdims=True))
           - jnp.exp(jnp.sum(lq2_ref[...] * lk2_ref[...], axis=-1, keepdims=True)) + lambda_init)
    o1 = acc_ref[0] / l_ref[0][:, :1]
    o2 = acc_ref[1] / l_ref[1][:, :1]
    o = o1 - lam * o2
    o = _rms(o, SUBLN_EPS) * sub_ref[...] * (1.0 - lambda_init)
    o_ref[...] = o.astype(o_ref.dtype)


def _attn_bounded_kernel(shift_ref, q_ref, k_ref, v_ref, km_ref, vm_ref, lq1_ref, lk1_ref, lq2_ref, lk2_ref,
                         sub_ref, o_ref, kt_ref, acc_ref, ls_ref, *, hd, n_meta, lambda_init):
    qi = pl.program_id(2)
    tq = q_ref.shape[0]
    n_kt, _, tk = kt_ref.shape
    shift = shift_ref[0, 0]

    @pl.when(qi == 0)
    def _():
        def transpose_tile(c, carry):
            r0 = pl.multiple_of(c * tk, tk)
            kt_ref[c] = k_ref[pl.ds(r0, tk), :].T
            return carry

        lax.fori_loop(0, n_kt, transpose_tile, 0)

    all_rows = slice(0, tq)

    def accumulate(mp, rows, s, v, mask, first):
        p = jnp.exp(s - shift)
        if mask is not None:
            p = jnp.where(mask, p, 0.0)
        part = p[:, 0:V7X_LANES]
        for c in range(1, p.shape[1] // V7X_LANES):
            part = part + p[:, c * V7X_LANES:(c + 1) * V7X_LANES]
        pv = jnp.dot(p.astype(BF16), v, preferred_element_type=F32)
        if first:
            ls_ref[mp, rows] = part
            acc_ref[mp, rows] = pv
        else:
            ls_ref[mp, rows] += part
            acc_ref[mp, rows] += pv

    def kv_tile(kt, rows, mask, first=False):
        k0 = pl.multiple_of(kt * tk, tk)
        v = v_ref[pl.ds(k0, tk), :]
        for mp in range(2):
            hs = slice(mp * hd, (mp + 1) * hd)
            s = jnp.dot(q_ref[rows, hs], kt_ref[kt, hs, :], preferred_element_type=F32)
            accumulate(mp, rows, s, v, mask, first)

    def causal(n_rows):
        row = lax.broadcasted_iota(jnp.int32, (n_rows, tk), 0)
        col = lax.broadcasted_iota(jnp.int32, (n_rows, tk), 1)
        return col <= row

    kv_tile(2 * qi, all_rows, causal(tq), first=True)

    def full_tiles(first, count):
        for t in range(count):
            kv_tile(first + t, all_rows, None)

    def four_full_tiles(quad, carry):
        full_tiles(4 * quad, 4)
        return carry

    lax.fori_loop(0, qi // 2, four_full_tiles, 0)

    @pl.when(qi % 2 == 1)
    def _():
        full_tiles(2 * qi - 2, 2)

    nt = (((1,), (1,)), ((), ()))
    meta_mask = lax.broadcasted_iota(jnp.int32, (tq, km_ref.shape[0]), 1) < n_meta
    for mp in range(2):
        hs = slice(mp * hd, (mp + 1) * hd)
        s = lax.dot_general(q_ref[:, hs], km_ref[:, hs], nt, preferred_element_type=F32)
        accumulate(mp, all_rows, s, vm_ref[...], meta_mask, False)
    kv_tile(2 * qi + 1, slice(tk, tq), causal(tq - tk))

    lam = (jnp.exp(jnp.sum(lq1_ref[...] * lk1_ref[...], axis=-1, keepdims=True))
           - jnp.exp(jnp.sum(lq2_ref[...] * lk2_ref[...], axis=-1, keepdims=True)) + lambda_init)
    o1 = acc_ref[0] / jnp.sum(ls_ref[0], axis=1, keepdims=True)
    o2 = acc_ref[1] / jnp.sum(ls_ref[1], axis=1, keepdims=True)
    o = o1 - lam * o2
    o = _rms(o, SUBLN_EPS) * sub_ref[...] * (1.0 - lambda_init)
    o_ref[...] = o.astype(o_ref.dtype)


SAFE_LOGIT_BOUND = 32.0


def _diff_attention(qkv, qkv_meta, logit_bound, lq1, lk1, lq2, lk2, subln, batch, n_meta, lambda_init):
    t, n3 = qkv.shape
    d = n3 // 3
    l = t // batch
    hw = 2 * HEAD_DIM
    heads = d // hw
    tq = _pick(l, 1024, 2 * V7X_LANES)
    tk = tq // 2
    nq = l // tq
    mrows = qkv_meta.shape[0]
    vec = pl.BlockSpec((1, HEAD_DIM), lambda b, h, i: (0, 0))
    blocks = [((tq, hw), BF16), ((l, hw), BF16), ((l, hw), BF16), ((mrows, hw), BF16), ((mrows, hw), BF16),
              ((tq, hw), BF16)]
    in_specs = [
        pl.BlockSpec((tq, hw), lambda b, h, i: (b * nq + i, h)),
        pl.BlockSpec((l, hw), lambda b, h, i: (b, heads + h)),
        pl.BlockSpec((l, hw), lambda b, h, i: (b, 2 * heads + h)),
        pl.BlockSpec((mrows, hw), lambda b, h, i: (0, heads + h)),
        pl.BlockSpec((mrows, hw), lambda b, h, i: (0, 2 * heads + h)),
        vec, vec, vec, vec,
        pl.BlockSpec((1, hw), lambda b, h, i: (0, 0)),
    ]
    common = dict(
        grid=(batch, heads, nq),
        out_specs=pl.BlockSpec((tq, hw), lambda b, h, i: (b * nq + i, h)),
        out_shape=jax.ShapeDtypeStruct((t, d), BF16),
    )
    operands = (qkv, qkv, qkv, qkv_meta, qkv_meta, lq1, lk1, lq2, lk2, subln)

    def running_max(_):
        kern = functools.partial(_attn_kernel, tk=tk, hd=HEAD_DIM, n_meta=n_meta, lambda_init=lambda_init)
        scratch = [((2, tq, hw), F32), ((2, tq, V7X_LANES), F32), ((2, tq, V7X_LANES), F32)]
        return pl.pallas_call(
            kern, in_specs=in_specs,
            scratch_shapes=[pltpu.VMEM(s, ty) for s, ty in scratch],
            compiler_params=_params(("arbitrary", "arbitrary", "arbitrary"), blocks, scratch),
            name="diff_attention_running_max", **common,
        )(*operands)

    def bounded(shift):
        kern = functools.partial(_attn_bounded_kernel, hd=HEAD_DIM, n_meta=n_meta, lambda_init=lambda_init)
        scratch = [((l // tk, hw, tk), BF16), ((2, tq, hw), F32), ((2, tq, V7X_LANES), F32)]
        return pl.pallas_call(
            kern,
            in_specs=[pl.BlockSpec((1, 1), lambda b, h, i: (0, 0), memory_space=pltpu.SMEM)] + in_specs,
            scratch_shapes=[pltpu.VMEM(s, ty) for s, ty in scratch],
            compiler_params=_params(("arbitrary", "arbitrary", "arbitrary"), blocks, scratch),
            name="diff_attention_bounded", **common,
        )(shift, *operands)

    shift = logit_bound.reshape(1, 1).astype(F32)
    return lax.cond(logit_bound <= SAFE_LOGIT_BOUND, bounded, running_max, shift)


def _router_kernel(x_ref, g_ref, r_ref, up_ref, rt_ref, *, n_exp):
    x = x_ref[...]
    u = _rms(x, RMS_EPS) * g_ref[...]
    u_hi = u.astype(BF16)
    u_lo = (u - u_hi.astype(F32)).astype(BF16)
    both = (jnp.dot(u_hi, r_ref[...], preferred_element_type=F32)
            + jnp.dot(u_lo, r_ref[...], preferred_element_type=F32))
    logits = both + pltpu.roll(both, V7X_LANES - n_exp, 1)
    lane = lax.broadcasted_iota(jnp.int32, logits.shape, 1).astype(F32)
    lg = jnp.where(lane < n_exp, logits, -jnp.inf)
    m1 = jnp.max(lg, axis=1, keepdims=True)
    i1 = jnp.min(jnp.where(lg == m1, lane, float(V7X_LANES)), axis=1, keepdims=True)
    lg2 = jnp.where(lane == i1, -jnp.inf, lg)
    m2 = jnp.max(lg2, axis=1, keepdims=True)
    i2 = jnp.min(jnp.where(lg2 == m2, lane, float(V7X_LANES)), axis=1, keepdims=True)
    e = jnp.exp(m2 - m1)
    g1 = 1.0 / (1.0 + e)
    g2 = e / (1.0 + e)
    rt = jnp.where(lane == 0, i1, jnp.where(lane == 1, i2, jnp.where(lane == 2, g1, jnp.where(lane == 3, g2, 0.0))))
    rt_ref[...] = rt
    up_ref[...] = _pack_bf16_pairs(u)


def _router(h, g, router_split, n_exp):
    m, d = h.shape
    bm = _pick(m, 256, 8)
    kern = functools.partial(_router_kernel, n_exp=n_exp)
    blocks = [((bm, d), F32), ((d, V7X_LANES), BF16), ((bm, d // 2), jnp.uint32), ((bm, V7X_LANES), F32)]
    return pl.pallas_call(
        kern,
        grid=(m // bm,),
        in_specs=[
            pl.BlockSpec((bm, d), lambda i: (i, 0)),
            pl.BlockSpec((1, d), lambda i: (0, 0)),
            pl.BlockSpec((d, V7X_LANES), lambda i: (0, 0)),
        ],
        out_specs=[pl.BlockSpec((bm, d // 2), lambda i: (i, 0)), pl.BlockSpec((bm, V7X_LANES), lambda i: (i, 0))],
        out_shape=[jax.ShapeDtypeStruct((m, d // 2), jnp.uint32), jax.ShapeDtypeStruct((m, V7X_LANES), F32)],
        compiler_params=_params(("arbitrary",), blocks),
        name="router",
    )(h, g, router_split)


ROW_DMA_UNROLL = 8


def _dispatch_kernel(pos_ref, fill_ref, u_ref, xs_ref, sem):
    bm = u_ref.shape[0]
    n_fill = fill_ref.shape[2]

    def row_copy(r, s):
        return pltpu.make_async_copy(u_ref.at[pl.ds(r, 1)], xs_ref.at[pl.ds(pos_ref[0, 0, 2 * r + s], 1)], sem)

    def fill_copy(k):
        return pltpu.make_async_copy(u_ref.at[pl.ds(k, 1)], xs_ref.at[pl.ds(fill_ref[0, 0, k], 1)], sem)

    def start(r, c):
        row_copy(r, 0).start(priority=0)
        row_copy(r, 1).start(priority=1)
        return c

    def wait(r, c):
        row_copy(r, 0).wait()
        row_copy(r, 1).wait()
        return c

    lax.fori_loop(0, bm, start, 0, unroll=ROW_DMA_UNROLL)
    lax.fori_loop(0, n_fill, lambda k, c: (fill_copy(k).start(), c)[1], 0, unroll=ROW_DMA_UNROLL)
    lax.fori_loop(0, bm, wait, 0, unroll=ROW_DMA_UNROLL)
    lax.fori_loop(0, n_fill, lambda k, c: (fill_copy(k).wait(), c)[1], 0, unroll=ROW_DMA_UNROLL)


def _dispatch(up, pos_tiles, fill_tiles, n_rows):
    m, d2 = up.shape
    bm = pos_tiles.shape[2] // TOP_K
    n_fill = fill_tiles.shape[2]
    assert fill_tiles.shape[0] == m // bm and n_fill <= bm
    blocks = [((bm, d2), jnp.uint32)]
    return pl.pallas_call(
        _dispatch_kernel,
        grid=(m // bm,),
        in_specs=[
            pl.BlockSpec((1, 1, TOP_K * bm), lambda i: (i, 0, 0), memory_space=pltpu.SMEM),
            pl.BlockSpec((1, 1, n_fill), lambda i: (i, 0, 0), memory_space=pltpu.SMEM),
            pl.BlockSpec((bm, d2), lambda i: (i, 0)),
        ],
        out_specs=pl.BlockSpec(memory_space=pl.ANY),
        out_shape=jax.ShapeDtypeStruct((n_rows, d2), jnp.uint32),
        scratch_shapes=[pltpu.SemaphoreType.DMA(())],
        compiler_params=_params(("arbitrary",), blocks),
        name="moe_dispatch",
    )(pos_tiles, fill_tiles, up)


def _combine_kernel(pos_ref, pos_next_ref, h_ref, rt_ref, y_ref, o_ref, buf_ref, sem, *, y_block):
    i = pl.program_id(0)
    bm = h_ref.shape[0]
    slot = i % 2

    def row_copy(p_ref, sl, r, s):
        return pltpu.make_async_copy(y_ref.at[pl.ds(p_ref[0, 0, 2 * r + s], 1)],
                                     buf_ref.at[sl, s, pl.ds(r, 1)], sem.at[sl])

    def start(p_ref, sl, r):
        row_copy(p_ref, sl, r, 0).start(priority=0)
        row_copy(p_ref, sl, r, 1).start(priority=1)

    def wait_all(p_ref, sl):
        def wait(r, c):
            row_copy(p_ref, sl, r, 0).wait()
            row_copy(p_ref, sl, r, 1).wait()
            return c

        lax.fori_loop(0, bm, wait, 0, unroll=ROW_DMA_UNROLL)

    @pl.when(i == 0)
    def _():
        lax.fori_loop(0, bm, lambda r, c: (start(pos_ref, 0, r), c)[1], 0, unroll=ROW_DMA_UNROLL)

    wait_all(pos_ref, slot)
    for r in range(bm):
        start(pos_next_ref, 1 - slot, r)
    rt = rt_ref[...]
    g1, g2 = rt[:, 2:3], rt[:, 3:4]
    half = y_block // 2
    for blk in range(h_ref.shape[1] // y_block):
        words = slice(blk * half, (blk + 1) * half)
        lo1, hi1 = _unpack_bf16_pairs(buf_ref[slot, 0, :, words])
        lo2, hi2 = _unpack_bf16_pairs(buf_ref[slot, 1, :, words])
        c_lo = slice(blk * y_block, blk * y_block + half)
        c_hi = slice(blk * y_block + half, (blk + 1) * y_block)
        o_ref[:, c_lo] = h_ref[:, c_lo] + g1 * lo1 + g2 * lo2
        o_ref[:, c_hi] = h_ref[:, c_hi] + g1 * hi1 + g2 * hi2

    @pl.when(i == pl.num_programs(0) - 1)
    def _():
        wait_all(pos_next_ref, 1 - slot)


def _combine(h, route, y, y_block, pos_tiles):
    m, d = h.shape
    n_tiles = pos_tiles.shape[0]
    bm = pos_tiles.shape[2] // TOP_K
    blocks = [((bm, d), F32), ((bm, V7X_LANES), F32), ((bm, d), F32)]
    scratch = [((2, TOP_K, bm, d // 2), jnp.uint32)]
    pos_spec = lambda index: pl.BlockSpec((1, 1, TOP_K * bm), index, memory_space=pltpu.SMEM)
    return pl.pallas_call(
        functools.partial(_combine_kernel, y_block=y_block),
        grid=(n_tiles,),
        in_specs=[
            pos_spec(lambda i: (i, 0, 0)),
            pos_spec(lambda i: (jnp.minimum(i + 1, n_tiles - 1), 0, 0)),
            pl.BlockSpec((bm, d), lambda i: (i, 0)),
            pl.BlockSpec((bm, V7X_LANES), lambda i: (i, 0)),
            pl.BlockSpec(memory_space=pl.ANY),
        ],
        out_specs=pl.BlockSpec((bm, d), lambda i: (i, 0)),
        out_shape=jax.ShapeDtypeStruct((m, d), F32),
        scratch_shapes=[pltpu.VMEM(*scratch[0]), pltpu.SemaphoreType.DMA((2,))],
        compiler_params=_params(("arbitrary",), blocks, scratch),
        name="moe_combine",
    )(pos_tiles, pos_tiles, h, route, y)


def _routing_tables(route, n_exp, tm, n_data_tiles):
    experts = route[:, :TOP_K].astype(jnp.int32).reshape(-1)
    ids = jnp.arange(n_exp, dtype=jnp.int32)
    onehot = (experts[:, None] == ids[None, :]).astype(jnp.int32)
    csum = jnp.cumsum(onehot, axis=0)
    rank = jnp.sum(csum * onehot, axis=1) - 1
    counts = csum[-1]
    padded = ((counts + tm - 1) // tm) * tm
    ends = jnp.cumsum(padded)
    starts = ends - padded
    pos = jnp.sum(starts[None, :] * onehot, axis=1) + rank
    pads = padded - counts
    pad_ends = jnp.cumsum(pads)
    f = jnp.arange(n_data_tiles * tm - experts.shape[0], dtype=jnp.int32)
    owner = jnp.sum((f[:, None] >= pad_ends[None, :]).astype(jnp.int32), axis=1)
    e = jnp.minimum(owner, n_exp - 1)
    in_padding = (starts + counts)[e] + f - (pad_ends - pads)[e]
    past_last = ends[-1] + f - pad_ends[-1]
    fill = jnp.where(owner < n_exp, in_padding, past_last)
    n_used = (ends[-1] // tm).astype(jnp.int32)
    tile_row = jnp.minimum(jnp.arange(n_data_tiles, dtype=jnp.int32), n_used - 1) * tm
    tile_expert = jnp.sum((tile_row[:, None] >= ends[None, :]).astype(jnp.int32), axis=1)
    return pos, fill, jnp.minimum(tile_expert, n_exp - 1).astype(jnp.int32), n_used.reshape(1)


def kernel(x, meta_tokens, norm_mix, norm_ffn, pool_w, pool_scale, ffn_w_gate, ffn_w_up, ffn_w_down,
           w_qkv, q_norm, k_norm, lambda_q1, lambda_k1, lambda_q2, lambda_k2, subln, w_o,
           router, exp_w_gate, exp_w_up, exp_w_down):
    batch, seq, d = x.shape
    n_meta = meta_tokens.shape[0]
    n_exp = router.shape[-1]
    assert norm_mix.shape[0] == 2, "one pooling layer followed by one attention layer"
    halo = max(POOL_WINDOWS)
    assert n_meta >= halo and n_meta <= V7X_LANES
    t = batch * seq
    lambda_init = 0.8 - 0.6 * math.exp(-0.3 * 1)

    row = lambda v: v.reshape(1, -1).astype(F32)
    pool_wb = pool_w[0].astype(BF16)
    wg, wu, wqkv = ffn_w_gate.astype(F32), ffn_w_up.astype(F32), w_qkv.astype(F32)
    wd, wo = ffn_w_down[0].astype(BF16), w_o[0].astype(BF16)
    ewg, ewu, ewd = (w.reshape(w.shape[1:]).astype(F32) for w in (exp_w_gate, exp_w_up, exp_w_down))
    r_hi = router[0].astype(BF16)
    r_lo = (router[0].astype(F32) - r_hi.astype(F32)).astype(BF16)
    router_split = jnp.pad(jnp.concatenate([r_hi, r_lo], axis=1), ((0, 0), (0, V7X_LANES - 2 * n_exp)))

    def layer0_and_qkv(tokens, first_rows, pos0):
        b, l, _ = tokens.shape
        h1, u2 = _pool_layer(tokens, first_rows, row(norm_mix[0]), pool_wb, row(pool_scale[0]),
                             row(norm_ffn[0]), pos0)
        a = _gateup(u2.reshape(b * l, d), wg, wu)
        h2, h2g, h2_ss = _matmul_residual(a, wd, h1.reshape(b * l, d), 3584, next_gain=row(norm_mix[1]))
        qkv = _qkv_proj(h2g, h2_ss, wqkv, row(q_norm[0]), row(k_norm[0]))
        return h2, qkv

    meta = meta_tokens.astype(F32)
    _, qkv_meta = layer0_and_qkv(meta[None], jnp.zeros((halo, d), F32), 0)
    qkv_meta = jnp.pad(qkv_meta, ((0, V7X_LANES - n_meta), (0, 0)))
    h2, qkv = layer0_and_qkv(x, meta[n_meta - halo:], n_meta)

    logit_bound = 1.01 * math.sqrt(HEAD_DIM) * jnp.max(jnp.abs(q_norm[0])) * jnp.max(jnp.abs(k_norm[0]))
    o = _diff_attention(qkv, qkv_meta, logit_bound.astype(F32), row(lambda_q1[0]), row(lambda_k1[0]),
                        row(lambda_q2[0]), row(lambda_k2[0]), row(subln[0]), batch, n_meta, lambda_init)
    h3 = _matmul_residual(o, wo, h2, d)

    up, route = _router(h3, row(norm_ffn[1]), router_split, n_exp)
    tm = _pick(t, 512, BF16_SUBLANES)
    n_data_tiles = (TOP_K * t) // tm + n_exp
    pos, fill, tile_expert, n_used = _routing_tables(route, n_exp, tm, n_data_tiles)
    bm_rows = _pick(t, 256, 8)
    n_steps = t // bm_rows
    assert (n_exp * tm) % n_steps == 0
    pos_tiles = pos.reshape(n_steps, 1, TOP_K * bm_rows)
    xs = _dispatch(up, pos_tiles, fill.reshape(n_steps, 1, -1), n_data_tiles * tm)
    runs = _runs_from_tile_expert(tile_expert, n_used, n_exp)
    a = _staged_matmul(xs, (ewg, ewu), runs, tm, _pick(ewg.shape[2], 512, V7X_LANES), True, "expert_gateup", "swiglu")
    y_block = _pick(d, 1024, 2 * V7X_LANES)
    y = _staged_matmul(a, (ewd,), runs, tm, y_block, False, "expert_down", "pack")
    out = _combine(h3, route, y, y_block, pos_tiles)
    return out.reshape(batch, seq, d)
```

```python
import functools
import math

import jax
import jax.numpy as jnp
from jax import lax
from jax.experimental import pallas as pl
from jax.experimental.pallas import tpu as pltpu

POOL_WINDOWS = (2, 4, 8, 16)
HEAD_DIM = 128
TOP_K = 2
RMS_EPS = 1e-6
SUBLN_EPS = 1e-5
MASK_VALUE = -1e30

V7X_LANES = 128
V7X_VMEM_BYTES = 64 * 1024 * 1024
V7X_VMEM_RESERVE_BYTES = 6 * 1024 * 1024
VMEM_TEMPORARIES_BYTES = 16 * 1024 * 1024
BF16_SUBLANES = 16
F32_SUBLANES = 8

F32 = jnp.float32
BF16 = jnp.bfloat16


def _pick(dim, target, mult):
    for b in range(min(dim, target), 0, -1):
        if dim % b == 0 and b % mult == 0:
            return b
    return dim


def _nbytes(shape, dtype):
    return math.prod(shape) * jnp.dtype(dtype).itemsize


def _params(semantics, pipelined, scratch=()):
    est = 2 * sum(_nbytes(s, d) for s, d in pipelined) + sum(_nbytes(s, d) for s, d in scratch)
    limit = min(V7X_VMEM_BYTES - V7X_VMEM_RESERVE_BYTES, est + VMEM_TEMPORARIES_BYTES)
    return pltpu.CompilerParams(dimension_semantics=semantics, vmem_limit_bytes=limit)


def _rms(v, eps):
    return v * lax.rsqrt(jnp.mean(v * v, axis=-1, keepdims=True) + eps)


def _pack_bf16_pairs(v):
    half = v.shape[1] // 2
    bits = lax.bitcast_convert_type(v.astype(BF16).astype(F32), jnp.uint32)
    return (bits[:, :half] >> 16) | (bits[:, half:] & jnp.uint32(0xFFFF0000))


def _unpack_bf16_pairs(w):
    return (lax.bitcast_convert_type(w << 16, F32),
            lax.bitcast_convert_type(w & jnp.uint32(0xFFFF0000), F32))


def _pool_kernel(x_ref, prev_ref, first_ref, gm_ref, pw_ref, ps_ref, gf_ref, h_ref, u_ref, ext_ref, lv_ref,
                 *, windows, halo, pos0):
    i = pl.program_id(1)
    tm = x_ref.shape[1]
    c = pw_ref.shape[1]
    pad = ext_ref.shape[0] - halo - tm
    top = pad + halo
    n = halo + tm
    x = x_ref[0]
    gm = gm_ref[...]
    before = jnp.where(i == 0, first_ref[...], prev_ref[0])
    ext_ref[0:pad] = jnp.zeros((pad, ext_ref.shape[1]), F32)
    lv_ref[:, 0:pad] = jnp.zeros((2, pad, c), F32)
    ext_ref[pad:top] = _rms(before, RMS_EPS) * gm
    ext_ref[top:] = _rms(x, RMS_EPS) * gm
    pos = pos0 + i * tm + lax.broadcasted_iota(jnp.int32, (tm, 1), 0)
    for g, w in enumerate(windows):
        cs = slice(g * c, (g + 1) * c)
        lv_ref[0, pad:] = ext_ref[pad:, cs] + ext_ref[pad - 1:pad - 1 + n, cs]
        cur, half = 0, 2
        while half < w:
            lv_ref[1 - cur, pad:] = lv_ref[cur, pad:] + lv_ref[cur, pad - half:pad - half + n]
            cur, half = 1 - cur, 2 * half
        inv_cnt = 1.0 / jnp.minimum(pos + 1, w).astype(F32)
        pooled = lv_ref[cur, top:] * inv_cnt - ext_ref[top:, cs]
        out = jnp.dot(pooled.astype(BF16), pw_ref[g], preferred_element_type=F32)
        h_ref[0, :, cs] = x[:, cs] + out * ps_ref[:, cs]
    h = h_ref[0]
    u_ref[0] = (_rms(h, RMS_EPS) * gf_ref[...]).astype(BF16)


def _pool_layer(x, first_rows, g_mix, pool_w, pool_scale, g_ffn, pos0):
    b, l, d = x.shape
    halo = max(POOL_WINDOWS)
    assert first_rows.shape == (halo, d)
    tm = _pick(l, 256, halo)
    nb = tm // halo
    g, c, _ = pool_w.shape
    kern = functools.partial(_pool_kernel, windows=POOL_WINDOWS, halo=halo, pos0=pos0)
    row = lambda bi, i: (bi, i, 0)
    const2 = lambda bi, i: (0, 0)
    blocks = [((1, tm, d), F32), ((1, halo, d), F32), ((halo, d), F32), ((g, c, c), BF16),
              ((1, tm, d), F32), ((1, tm, d), BF16)]
    assert all(w >= 2 and w & (w - 1) == 0 for w in POOL_WINDOWS)
    rows = F32_SUBLANES + halo + tm
    scratch = [((rows, d), F32), ((2, rows, c), F32)]
    return pl.pallas_call(
        kern,
        grid=(b, l // tm),
        in_specs=[
            pl.BlockSpec((1, tm, d), row),
            pl.BlockSpec((1, halo, d), lambda bi, i: (bi, jnp.maximum(i * nb - 1, 0), 0)),
            pl.BlockSpec((halo, d), const2),
            pl.BlockSpec((1, d), const2),
            pl.BlockSpec((g, c, c), lambda bi, i: (0, 0, 0)),
            pl.BlockSpec((1, d), const2),
            pl.BlockSpec((1, d), const2),
        ],
        out_specs=[pl.BlockSpec((1, tm, d), row), pl.BlockSpec((1, tm, d), row)],
        out_shape=[jax.ShapeDtypeStruct((b, l, d), F32), jax.ShapeDtypeStruct((b, l, d), BF16)],
        scratch_shapes=[pltpu.VMEM(s, t) for s, t in scratch],
        compiler_params=_params(("arbitrary", "arbitrary"), blocks, scratch),
        name="pool_layer",
    )(x, x, first_rows, g_mix, pool_w, pool_scale, g_ffn)


def _runs_from_tile_expert(tile_expert, n_used, n_exp):
    te = tile_expert
    prev = jnp.concatenate([te[:1] - 1, te[:-1]])
    start = (te != prev).astype(jnp.int32)
    ids = jnp.arange(n_exp, dtype=jnp.int32)
    present = jnp.any(te[None, :] == ids[:, None], axis=1)
    later = jnp.logical_and(present[None, :], ids[None, :] > ids[:, None])
    next_of = jnp.min(jnp.where(later, ids[None, :], n_exp), axis=1)
    next_of = jnp.where(next_of == n_exp, te[0], next_of).astype(jnp.int32)
    return te, n_used, start, next_of[te], (te == te[-1]).astype(jnp.int32)


def _qkv_epilogue(j, x, wb_ref, ss_ref, qn_ref, kn_ref, o_ref, *, n_q_blocks, hd):
    row_scale = lax.rsqrt(jnp.sum(ss_ref[...], axis=1, keepdims=True) * (1.0 / x.shape[1]) + RMS_EPS)
    is_qk = j < 2 * n_q_blocks
    gain = jnp.where(j < n_q_blocks, qn_ref[...] * hd ** -0.5, kn_ref[...])
    for c in range(o_ref.shape[1] // (2 * hd)):
        acc = jnp.dot(x, wb_ref[0, :, c * 2 * hd:(c + 1) * 2 * hd], preferred_element_type=F32) * row_scale
        for mp in range(2):
            blk = acc[:, mp * hd:(mp + 1) * hd]
            cs = slice((2 * c + mp) * hd, (2 * c + mp + 1) * hd)
            o_ref[:, cs] = jnp.where(is_qk, _rms(blk, RMS_EPS) * gain, blk).astype(o_ref.dtype)


def _staged_kernel(te_ref, nu_ref, st_ref, nx_ref, lr_ref, x_ref, *rest, n_w, n_extra, packed, epilogue):
    w_hbm = rest[:n_w]
    extras = rest[n_w:n_w + n_extra]
    o_ref, stage_ref, wb_ref, sem = rest[n_w + n_extra:]
    j = pl.program_id(0)
    i = pl.program_id(1)
    n_stage, _, _, bn = stage_ref.shape
    used = i < nu_ref[0]

    def copies(e, jj, slot):
        c0 = pl.multiple_of(jj * bn, bn)
        return [pltpu.make_async_copy(w.at[e, :, pl.ds(c0, bn)], stage_ref.at[slot, n], sem.at[slot, n])
                for n, w in enumerate(w_hbm)]

    @pl.when(jnp.logical_and(used, st_ref[i] == 1))
    def _():
        slot = j % n_stage
        j_next = j + lr_ref[i]

        def request_next():
            @pl.when(j_next < pl.num_programs(0))
            def _():
                for c in copies(nx_ref[i], j_next, j_next % n_stage):
                    c.start()

        @pl.when(jnp.logical_and(j == 0, i == 0))
        def _():
            for c in copies(te_ref[0], 0, 0):
                c.start()

        if n_stage == 2:
            request_next()
        for c in copies(te_ref[i], j, slot):
            c.wait()
        wb_ref[...] = stage_ref[slot].astype(BF16)
        if n_stage == 1:
            request_next()

    @pl.when(used)
    def _():
        if packed:
            d2 = x_ref.shape[1]
            lo, hi = _unpack_bf16_pairs(x_ref[...])
            parts = [(lo.astype(BF16), slice(0, d2)), (hi.astype(BF16), slice(d2, 2 * d2))]
        else:
            parts = [(x_ref[...], slice(None))]
        if callable(epilogue):
            epilogue(j, parts[0][0], wb_ref, *extras, o_ref)
            return
        prods = [sum(jnp.dot(xp, wb_ref[n, ks], preferred_element_type=F32) for xp, ks in parts)
                 for n in range(n_w)]
        if epilogue == "swiglu":
            g, u = prods
            o_ref[...] = (g * jax.nn.sigmoid(g) * u).astype(o_ref.dtype)
        else:
            o_ref[...] = _pack_bf16_pairs(prods[0])

    @pl.when(jnp.logical_not(used))
    def _():
        o_ref[...] = jnp.zeros_like(o_ref)


def _staged_matmul(x, weights, runs, tm, bn, packed, name, epilogue, extras=(), extra_specs=(), extra_blocks=(),
                   single_run=False):
    rows = x.shape[0]
    _, k, n = weights[0].shape
    n_w = len(weights)
    kern = functools.partial(_staged_kernel, n_w=n_w, n_extra=len(extras), packed=packed, epilogue=epilogue)
    out_cols, out_bn, out_dtype = (n // 2, bn // 2, jnp.uint32) if epilogue == "pack" else (n, bn, BF16)
    blocks = [((tm, x.shape[1]), x.dtype), ((tm, out_bn), out_dtype), *extra_blocks]
    stage, cast = ((n_w, k, bn), F32), ((n_w, k, bn), BF16)
    fits_two = (2 * sum(_nbytes(*b) for b in blocks) + 2 * _nbytes(*stage) + _nbytes(*cast)
                + VMEM_TEMPORARIES_BYTES <= V7X_VMEM_BYTES - V7X_VMEM_RESERVE_BYTES)
    n_stage = 2 if (single_run and fits_two) else 1
    scratch = [((n_stage, n_w, k, bn), F32), cast]
    grid_spec = pltpu.PrefetchScalarGridSpec(
        num_scalar_prefetch=5,
        grid=(n // bn, rows // tm),
        in_specs=[pl.BlockSpec((tm, x.shape[1]), lambda j, i, te, nu, *_: (jnp.minimum(i, nu[0] - 1), 0))]
        + [pl.BlockSpec(memory_space=pl.ANY)] * n_w + list(extra_specs),
        out_specs=pl.BlockSpec((tm, out_bn), lambda j, i, *_: (i, j)),
        scratch_shapes=[pltpu.VMEM(s, t) for s, t in scratch] + [pltpu.SemaphoreType.DMA((n_stage, n_w))],
    )
    return pl.pallas_call(
        kern,
        grid_spec=grid_spec,
        out_shape=jax.ShapeDtypeStruct((rows, out_cols), out_dtype),
        compiler_params=_params(("arbitrary", "arbitrary"), blocks, scratch),
        name=name,
    )(*runs, x, *weights, *extras)


def _single_run(m, bm):
    n_tiles = m // bm
    return _runs_from_tile_expert(jnp.zeros((n_tiles,), jnp.int32), jnp.full((1,), n_tiles, jnp.int32), 1)


def _gateup(x, wg, wu):
    m = x.shape[0]
    bm = _pick(m, 1024, BF16_SUBLANES)
    bn = _pick(wg.shape[2], 512, V7X_LANES)
    return _staged_matmul(x, (wg, wu), _single_run(m, bm), bm, bn, False, "ffn_gateup", "swiglu", single_run=True)


def _matmul_residual_kernel(a_ref, w_ref, r_ref, *rest, for_next_norm):
    if for_next_norm:
        g_ref, o_ref, og_ref, ss_ref = rest
    else:
        (o_ref,) = rest
    j = pl.program_id(1)
    kk = pl.program_id(2)
    part = jnp.dot(a_ref[...], w_ref[...], preferred_element_type=F32)

    @pl.when(kk == 0)
    def _():
        o_ref[...] = r_ref[...] + part

    @pl.when(kk != 0)
    def _():
        o_ref[...] += part

    if for_next_norm:
        @pl.when(kk == pl.num_programs(2) - 1)
        def _():
            o = o_ref[...]
            og_ref[...] = (o * g_ref[...]).astype(og_ref.dtype)
            sq = o * o
            lane_tiles = [sq[:, c * V7X_LANES:(c + 1) * V7X_LANES] for c in range(sq.shape[1] // V7X_LANES)]
            partial = functools.reduce(lambda x, y: x + y, lane_tiles)

            @pl.when(j == 0)
            def _():
                ss_ref[...] = partial

            @pl.when(j != 0)
            def _():
                ss_ref[...] += partial


def _matmul_residual(a, w, r, bk_target, next_gain=None):
    m, k = a.shape
    n = w.shape[1]
    for_next_norm = next_gain is not None
    bm = _pick(m, 1024, BF16_SUBLANES)
    bn = _pick(n, 1024, V7X_LANES)
    bk = _pick(k, bk_target, V7X_LANES)
    blocks = [((bm, bk), BF16), ((bk, bn), BF16), ((bm, bn), F32), ((bm, bn), F32)]
    in_specs = [
        pl.BlockSpec((bm, bk), lambda i, j, kk: (i, kk)),
        pl.BlockSpec((bk, bn), lambda i, j, kk: (kk, j)),
        pl.BlockSpec((bm, bn), lambda i, j, kk: (i, j)),
    ]
    out_specs = [pl.BlockSpec((bm, bn), lambda i, j, kk: (i, j))]
    out_shape = [jax.ShapeDtypeStruct((m, n), F32)]
    operands = [a, w, r]
    if for_next_norm:
        blocks += [((bm, bn), BF16), ((bm, V7X_LANES), F32)]
        in_specs.append(pl.BlockSpec((1, bn), lambda i, j, kk: (0, j)))
        out_specs += [pl.BlockSpec((bm, bn), lambda i, j, kk: (i, j)),
                      pl.BlockSpec((bm, V7X_LANES), lambda i, j, kk: (i, 0))]
        out_shape += [jax.ShapeDtypeStruct((m, n), BF16), jax.ShapeDtypeStruct((m, V7X_LANES), F32)]
        operands.append(next_gain)
    out = pl.pallas_call(
        functools.partial(_matmul_residual_kernel, for_next_norm=for_next_norm),
        grid=(m // bm, n // bn, k // bk),
        in_specs=in_specs,
        out_specs=out_specs,
        out_shape=out_shape,
        compiler_params=_params(("arbitrary", "arbitrary", "arbitrary"), blocks),
        name="matmul_residual",
    )(*operands)
    return out if for_next_norm else out[0]


def _qkv_proj(hg, ss, w, q_norm, k_norm):
    m, d = hg.shape
    bm = _pick(m, 1024, BF16_SUBLANES)
    bn = _pick(d, 1024, 2 * HEAD_DIM)
    head = pl.BlockSpec((1, HEAD_DIM), lambda j, i, *_: (0, 0))
    return _staged_matmul(
        hg, (w,), _single_run(m, bm), bm, bn, False, "qkv_proj",
        functools.partial(_qkv_epilogue, n_q_blocks=d // bn, hd=HEAD_DIM),
        extras=(ss, q_norm, k_norm),
        extra_specs=(pl.BlockSpec((bm, V7X_LANES), lambda j, i, *_: (i, 0)), head, head),
        extra_blocks=(((bm, V7X_LANES), F32),),
        single_run=True,
    )


def _attn_kernel(q_ref, k_ref, v_ref, km_ref, vm_ref, lq1_ref, lk1_ref, lq2_ref, lk2_ref, sub_ref,
                 o_ref, acc_ref, m_ref, l_ref, *, tk, hd, n_meta, lambda_init):
    qi = pl.program_id(2)
    tq = q_ref.shape[0]
    nt = (((1,), (1,)), ((), ()))

    def update(mp, k, v, mask, first):
        q = q_ref[:, mp * hd:(mp + 1) * hd]
        s = lax.dot_general(q, k, nt, preferred_element_type=F32)
        if mask is not None:
            s = jnp.where(mask, s, MASK_VALUE)
        row_max = jnp.max(s, axis=1, keepdims=True)
        if first:
            m_new = row_max
        else:
            m_old = m_ref[mp][:, :1]
            m_new = jnp.maximum(m_old, row_max)
        p = jnp.exp(s - m_new)
        row_sum = jnp.sum(p, axis=1, keepdims=True)
        pv = jnp.dot(p.astype(BF16), v, preferred_element_type=F32)
        if first:
            l_new = row_sum
            acc_ref[mp] = pv
        else:
            alpha = jnp.exp(m_old - m_new)
            l_new = alpha * l_ref[mp][:, :1] + row_sum
            acc_ref[mp] = alpha * acc_ref[mp] + pv
        m_ref[mp] = jnp.broadcast_to(m_new, (tq, V7X_LANES))
        l_ref[mp] = jnp.broadcast_to(l_new, (tq, V7X_LANES))

    meta_mask = lax.broadcasted_iota(jnp.int32, (tq, km_ref.shape[0]), 1) < n_meta
    for mp in range(2):
        update(mp, km_ref[:, mp * hd:(mp + 1) * hd], vm_ref[...], meta_mask, True)

    n_full = (qi * tq) // tk

    def full_tile(kt, carry):
        k0 = pl.multiple_of(kt * tk, tk)
        for mp in range(2):
            update(mp, k_ref[pl.ds(k0, tk), mp * hd:(mp + 1) * hd], v_ref[pl.ds(k0, tk), :], None, False)
        return carry

    lax.fori_loop(0, n_full, full_tile, 0)

    row = lax.broadcasted_iota(jnp.int32, (tq, tk), 0)
    col = lax.broadcasted_iota(jnp.int32, (tq, tk), 1)
    for dt in range(tq // tk):
        k0 = pl.multiple_of(qi * tq + dt * tk, tk)
        diag_mask = col + dt * tk <= row
        for mp in range(2):
            update(mp, k_ref[pl.ds(k0, tk), mp * hd:(mp + 1) * hd], v_ref[pl.ds(k0, tk), :], diag_mask, False)

    lam = (jnp.exp(jnp.sum(lq1_ref[...] * lk1_ref[...], axis=-1, keepdims=True))
           - jnp.exp(jnp.sum(lq2_ref[...] * lk2_ref[...], axis=-1, keepdims=True)) + lambda_init)
    o1 = acc_ref[0] / l_ref[0][:, :1]
    o2 = acc_ref[1] / l_ref[1][:, :1]
    o = o1 - lam * o2
    o = _rms(o, SUBLN_EPS) * sub_ref[...] * (1.0 - lambda_init)
    o_ref[...] = o.astype(o_ref.dtype)


def _attn_bounded_kernel(shift_ref, q_ref, k_ref, v_ref, km_ref, vm_ref, lq1_ref, lk1_ref, lq2_ref, lk2_ref,
                         sub_ref, o_ref, kt_ref, acc_ref, ls_ref, *, hd, n_meta, lambda_init):
    qi = pl.program_id(2)
    tq = q_ref.shape[0]
    n_kt, _, tk = kt_ref.shape
    shift = shift_ref[0, 0]

    @pl.when(qi == 0)
    def _():
        def transpose_tile(c, carry):
            r0 = pl.multiple_of(c * tk, tk)
            kt_ref[c] = k_ref[pl.ds(r0, tk), :].T
            return carry

        lax.fori_loop(0, n_kt, transpose_tile, 0)

    all_rows = slice(0, tq)

    def accumulate(mp, rows, s, v, mask, first):
        p = jnp.exp(s - shift)
        if mask is not None:
            p = jnp.where(mask, p, 0.0)
        part = p[:, 0:V7X_LANES]
        for c in range(1, p.shape[1] // V7X_LANES):
            part = part + p[:, c * V7X_LANES:(c + 1) * V7X_LANES]
        pv = jnp.dot(p.astype(BF16), v, preferred_element_type=F32)
        if first:
            ls_ref[mp, rows] = part
            acc_ref[mp, rows] = pv
        else:
            ls_ref[mp, rows] += part
            acc_ref[mp, rows] += pv

    def kv_tile(kt, rows, mask, first=False):
        k0 = pl.multiple_of(kt * tk, tk)
        v = v_ref[pl.ds(k0, tk), :]
        for mp in range(2):
            hs = slice(mp * hd, (mp + 1) * hd)
            s = jnp.dot(q_ref[rows, hs], kt_ref[kt, hs, :], preferred_element_type=F32)
            accumulate(mp, rows, s, v, mask, first)

    def causal(n_rows):
        row = lax.broadcasted_iota(jnp.int32, (n_rows, tk), 0)
        col = lax.broadcasted_iota(jnp.int32, (n_rows, tk), 1)
        return col <= row

    kv_tile(2 * qi, all_rows, causal(tq), first=True)

    def full_tiles(first, count):
        for t in range(count):
            kv_tile(first + t, all_rows, None)

    def four_full_tiles(quad, carry):
        full_tiles(4 * quad, 4)
        return carry

    lax.fori_loop(0, qi // 2, four_full_tiles, 0)

    @pl.when(qi % 2 == 1)
    def _():
        full_tiles(2 * qi - 2, 2)

    nt = (((1,), (1,)), ((), ()))
    meta_mask = lax.broadcasted_iota(jnp.int32, (tq, km_ref.shape[0]), 1) < n_meta
    for mp in range(2):
        hs = slice(mp * hd, (mp + 1) * hd)
        s = lax.dot_general(q_ref[:, hs], km_ref[:, hs], nt, preferred_element_type=F32)
        accumulate(mp, all_rows, s, vm_ref[...], meta_mask, False)
    kv_tile(2 * qi + 1, slice(tk, tq), causal(tq - tk))

    lam = (jnp.exp(jnp.sum(lq1_ref[...] * lk1_ref[...], axis=-1, keepdims=True))
           - jnp.exp(jnp.sum(lq2_ref[...] * lk2_ref[...], axis=-1, keepdims=True)) + lambda_init)
    o1 = acc_ref[0] / jnp.sum(ls_ref[0], axis=1, keepdims=True)
    o2 = acc_ref[1] / jnp.sum(ls_ref[1], axis=1, keepdims=True)
    o = o1 - lam * o2
    o = _rms(o, SUBLN_EPS) * sub_ref[...] * (1.0 - lambda_init)
    o_ref[...] = o.astype(o_ref.dtype)


SAFE_LOGIT_BOUND = 32.0


def _diff_attention(qkv, qkv_meta, logit_bound, lq1, lk1, lq2, lk2, subln, batch, n_meta, lambda_init):
    t, n3 = qkv.shape
    d = n3 // 3
    l = t // batch
    hw = 2 * HEAD_DIM
    heads = d // hw
    tq = _pick(l, 1024, 2 * V7X_LANES)
    tk = tq // 2
    nq = l // tq
    mrows = qkv_meta.shape[0]
    vec = pl.BlockSpec((1, HEAD_DIM), lambda b, h, i: (0, 0))
    blocks = [((tq, hw), BF16), ((l, hw), BF16), ((l, hw), BF16), ((mrows, hw), BF16), ((mrows, hw), BF16),
              ((tq, hw), BF16)]
    in_specs = [
        pl.BlockSpec((tq, hw), lambda b, h, i: (b * nq + i, h)),
        pl.BlockSpec((l, hw), lambda b, h, i: (b, heads + h)),
        pl.BlockSpec((l, hw), lambda b, h, i: (b, 2 * heads + h)),
        pl.BlockSpec((mrows, hw), lambda b, h, i: (0, heads + h)),
        pl.BlockSpec((mrows, hw), lambda b, h, i: (0, 2 * heads + h)),
        vec, vec, vec, vec,
        pl.BlockSpec((1, hw), lambda b, h, i: (0, 0)),
    ]
    common = dict(
        grid=(batch, heads, nq),
        out_specs=pl.BlockSpec((tq, hw), lambda b, h, i: (b * nq + i, h)),
        out_shape=jax.ShapeDtypeStruct((t, d), BF16),
    )
    operands = (qkv, qkv, qkv, qkv_meta, qkv_meta, lq1, lk1, lq2, lk2, subln)

    def running_max(_):
        kern = functools.partial(_attn_kernel, tk=tk, hd=HEAD_DIM, n_meta=n_meta, lambda_init=lambda_init)
        scratch = [((2, tq, hw), F32), ((2, tq, V7X_LANES), F32), ((2, tq, V7X_LANES), F32)]
        return pl.pallas_call(
            kern, in_specs=in_specs,
            scratch_shapes=[pltpu.VMEM(s, ty) for s, ty in scratch],
            compiler_params=_params(("arbitrary", "arbitrary", "arbitrary"), blocks, scratch),
            name="diff_attention_running_max", **common,
        )(*operands)

    def bounded(shift):
        kern = functools.partial(_attn_bounded_kernel, hd=HEAD_DIM, n_meta=n_meta, lambda_init=lambda_init)
        scratch = [((l // tk, hw, tk), BF16), ((2, tq, hw), F32), ((2, tq, V7X_LANES), F32)]
        return pl.pallas_call(
            kern,
            in_specs=[pl.BlockSpec((1, 1), lambda b, h, i: (0, 0), memory_space=pltpu.SMEM)] + in_specs,
            scratch_shapes=[pltpu.VMEM(s, ty) for s, ty in scratch],
            compiler_params=_params(("arbitrary", "arbitrary", "arbitrary"), blocks, scratch),
            name="diff_attention_bounded", **common,
        )(shift, *operands)

    shift = logit_bound.reshape(1, 1).astype(F32)
    return lax.cond(logit_bound <= SAFE_LOGIT_BOUND, bounded, running_max, shift)


def _router_kernel(x_ref, g_ref, r_ref, up_ref, rt_ref, *, n_exp):
    x = x_ref[...]
    u = _rms(x, RMS_EPS) * g_ref[...]
    u_hi = u.astype(BF16)
    u_lo = (u - u_hi.astype(F32)).astype(BF16)
    both = (jnp.dot(u_hi, r_ref[...], preferred_element_type=F32)
            + jnp.dot(u_lo, r_ref[...], preferred_element_type=F32))
    logits = both + pltpu.roll(both, V7X_LANES - n_exp, 1)
    lane = lax.broadcasted_iota(jnp.int32, logits.shape, 1).astype(F32)
    lg = jnp.where(lane < n_exp, logits, -jnp.inf)
    m1 = jnp.max(lg, axis=1, keepdims=True)
    i1 = jnp.min(jnp.where(lg == m1, lane, float(V7X_LANES)), axis=1, keepdims=True)
    lg2 = jnp.where(lane == i1, -jnp.inf, lg)
    m2 = jnp.max(lg2, axis=1, keepdims=True)
    i2 = jnp.min(jnp.where(lg2 == m2, lane, float(V7X_LANES)), axis=1, keepdims=True)
    e = jnp.exp(m2 - m1)
    g1 = 1.0 / (1.0 + e)
    g2 = e / (1.0 + e)
    rt = jnp.where(lane == 0, i1, jnp.where(lane == 1, i2, jnp.where(lane == 2, g1, jnp.where(lane == 3, g2, 0.0))))
    rt_ref[...] = rt
    up_ref[...] = _pack_bf16_pairs(u)


def _router(h, g, router_split, n_exp):
    m, d = h.shape
    bm = _pick(m, 512, 8)
    kern = functools.partial(_router_kernel, n_exp=n_exp)
    blocks = [((bm, d), F32), ((d, V7X_LANES), BF16), ((bm, d // 2), jnp.uint32), ((bm, V7X_LANES), F32)]
    return pl.pallas_call(
        kern,
        grid=(m // bm,),
        in_specs=[
            pl.BlockSpec((bm, d), lambda i: (i, 0)),
            pl.BlockSpec((1, d), lambda i: (0, 0)),
            pl.BlockSpec((d, V7X_LANES), lambda i: (0, 0)),
        ],
        out_specs=[pl.BlockSpec((bm, d // 2), lambda i: (i, 0)), pl.BlockSpec((bm, V7X_LANES), lambda i: (i, 0))],
        out_shape=[jax.ShapeDtypeStruct((m, d // 2), jnp.uint32), jax.ShapeDtypeStruct((m, V7X_LANES), F32)],
        compiler_params=_params(("arbitrary",), blocks),
        name="router",
    )(h, g, router_split)


ROW_DMA_UNROLL = 8


def _dispatch_kernel(pos_ref, fill_ref, u_ref, xs_ref, sem):
    bm = u_ref.shape[0]
    n_fill = fill_ref.shape[2]

    def row_copy(r, s):
        return pltpu.make_async_copy(u_ref.at[pl.ds(r, 1)], xs_ref.at[pl.ds(pos_ref[0, 0, 2 * r + s], 1)], sem)

    def fill_copy(k):
        return pltpu.make_async_copy(u_ref.at[pl.ds(k, 1)], xs_ref.at[pl.ds(fill_ref[0, 0, k], 1)], sem)

    def start(r, c):
        row_copy(r, 0).start(priority=0)
        row_copy(r, 1).start(priority=1)
        return c

    def wait(r, c):
        row_copy(r, 0).wait()
        row_copy(r, 1).wait()
        return c

    lax.fori_loop(0, bm, start, 0, unroll=ROW_DMA_UNROLL)
    lax.fori_loop(0, n_fill, lambda k, c: (fill_copy(k).start(), c)[1], 0, unroll=ROW_DMA_UNROLL)
    lax.fori_loop(0, bm, wait, 0, unroll=ROW_DMA_UNROLL)
    lax.fori_loop(0, n_fill, lambda k, c: (fill_copy(k).wait(), c)[1], 0, unroll=ROW_DMA_UNROLL)


def _dispatch(up, pos_tiles, fill_tiles, n_rows):
    m, d2 = up.shape
    bm = pos_tiles.shape[2] // TOP_K
    n_fill = fill_tiles.shape[2]
    assert fill_tiles.shape[0] == m // bm and n_fill <= bm
    blocks = [((bm, d2), jnp.uint32)]
    return pl.pallas_call(
        _dispatch_kernel,
        grid=(m // bm,),
        in_specs=[
            pl.BlockSpec((1, 1, TOP_K * bm), lambda i: (i, 0, 0), memory_space=pltpu.SMEM),
            pl.BlockSpec((1, 1, n_fill), lambda i: (i, 0, 0), memory_space=pltpu.SMEM),
            pl.BlockSpec((bm, d2), lambda i: (i, 0)),
        ],
        out_specs=pl.BlockSpec(memory_space=pl.ANY),
        out_shape=jax.ShapeDtypeStruct((n_rows, d2), jnp.uint32),
        scratch_shapes=[pltpu.SemaphoreType.DMA(())],
        compiler_params=_params(("arbitrary",), blocks),
        name="moe_dispatch",
    )(pos_tiles, fill_tiles, up)


def _combine_kernel(pos_ref, pos_next_ref, h_ref, rt_ref, y_ref, o_ref, buf_ref, sem, *, y_block):
    i = pl.program_id(0)
    bm = h_ref.shape[0]
    slot = i % 2

    def row_copy(p_ref, sl, r, s):
        return pltpu.make_async_copy(y_ref.at[pl.ds(p_ref[0, 0, 2 * r + s], 1)],
                                     buf_ref.at[sl, s, pl.ds(r, 1)], sem.at[sl])

    def start(p_ref, sl, r):
        row_copy(p_ref, sl, r, 0).start(priority=0)
        row_copy(p_ref, sl, r, 1).start(priority=1)

    def wait_all(p_ref, sl):
        def wait(r, c):
            row_copy(p_ref, sl, r, 0).wait()
            row_copy(p_ref, sl, r, 1).wait()
            return c

        lax.fori_loop(0, bm, wait, 0, unroll=ROW_DMA_UNROLL)

    @pl.when(i == 0)
    def _():
        lax.fori_loop(0, bm, lambda r, c: (start(pos_ref, 0, r), c)[1], 0, unroll=ROW_DMA_UNROLL)

    wait_all(pos_ref, slot)
    for r in range(bm):
        start(pos_next_ref, 1 - slot, r)
    rt = rt_ref[...]
    g1, g2 = rt[:, 2:3], rt[:, 3:4]
    half = y_block // 2
    for blk in range(h_ref.shape[1] // y_block):
        words = slice(blk * half, (blk + 1) * half)
        lo1, hi1 = _unpack_bf16_pairs(buf_ref[slot, 0, :, words])
        lo2, hi2 = _unpack_bf16_pairs(buf_ref[slot, 1, :, words])
        c_lo = slice(blk * y_block, blk * y_block + half)
        c_hi = slice(blk * y_block + half, (blk + 1) * y_block)
        o_ref[:, c_lo] = h_ref[:, c_lo] + g1 * lo1 + g2 * lo2
        o_ref[:, c_hi] = h_ref[:, c_hi] + g1 * hi1 + g2 * hi2

    @pl.when(i == pl.num_programs(0) - 1)
    def _():
        wait_all(pos_next_ref, 1 - slot)


def _combine(h, route, y, y_block, pos_tiles):
    m, d = h.shape
    n_tiles = pos_tiles.shape[0]
    bm = pos_tiles.shape[2] // TOP_K
    blocks = [((bm, d), F32), ((bm, V7X_LANES), F32), ((bm, d), F32)]
    scratch = [((2, TOP_K, bm, d // 2), jnp.uint32)]
    pos_spec = lambda index: pl.BlockSpec((1, 1, TOP_K * bm), index, memory_space=pltpu.SMEM)
    return pl.pallas_call(
        functools.partial(_combine_kernel, y_block=y_block),
        grid=(n_tiles,),
        in_specs=[
            pos_spec(lambda i: (i, 0, 0)),
            pos_spec(lambda i: (jnp.minimum(i + 1, n_tiles - 1), 0, 0)),
            pl.BlockSpec((bm, d), lambda i: (i, 0)),
            pl.BlockSpec((bm, V7X_LANES), lambda i: (i, 0)),
            pl.BlockSpec(memory_space=pl.ANY),
        ],
        out_specs=pl.BlockSpec((bm, d), lambda i: (i, 0)),
        out_shape=jax.ShapeDtypeStruct((m, d), F32),
        scratch_shapes=[pltpu.VMEM(*scratch[0]), pltpu.SemaphoreType.DMA((2,))],
        compiler_params=_params(("arbitrary",), blocks, scratch),
        name="moe_combine",
    )(pos_tiles, pos_tiles, h, route, y)


def _routing_tables(route, n_exp, tm, n_data_tiles):
    experts = route[:, :TOP_K].astype(jnp.int32).reshape(-1)
    ids = jnp.arange(n_exp, dtype=jnp.int32)
    onehot = (experts[:, None] == ids[None, :]).astype(jnp.int32)
    csum = jnp.cumsum(onehot, axis=0)
    rank = jnp.sum(csum * onehot, axis=1) - 1
    counts = csum[-1]
    padded = ((counts + tm - 1) // tm) * tm
    ends = jnp.cumsum(padded)
    starts = ends - padded
    pos = jnp.sum(starts[None, :] * onehot, axis=1) + rank
    pads = padded - counts
    pad_ends = jnp.cumsum(pads)
    f = jnp.arange(n_data_tiles * tm - experts.shape[0], dtype=jnp.int32)
    owner = jnp.sum((f[:, None] >= pad_ends[None, :]).astype(jnp.int32), axis=1)
    e = jnp.minimum(owner, n_exp - 1)
    in_padding = (starts + counts)[e] + f - (pad_ends - pads)[e]
    past_last = ends[-1] + f - pad_ends[-1]
    fill = jnp.where(owner < n_exp, in_padding, past_last)
    n_used = (ends[-1] // tm).astype(jnp.int32)
    tile_row = jnp.minimum(jnp.arange(n_data_tiles, dtype=jnp.int32), n_used - 1) * tm
    tile_expert = jnp.sum((tile_row[:, None] >= ends[None, :]).astype(jnp.int32), axis=1)
    return pos, fill, jnp.minimum(tile_expert, n_exp - 1).astype(jnp.int32), n_used.reshape(1)


def kernel(x, meta_tokens, norm_mix, norm_ffn, pool_w, pool_scale, ffn_w_gate, ffn_w_up, ffn_w_down,
           w_qkv, q_norm, k_norm, lambda_q1, lambda_k1, lambda_q2, lambda_k2, subln, w_o,
           router, exp_w_gate, exp_w_up, exp_w_down):
    batch, seq, d = x.shape
    n_meta = meta_tokens.shape[0]
    n_exp = router.shape[-1]
    assert norm_mix.shape[0] == 2, "one pooling layer followed by one attention layer"
    halo = max(POOL_WINDOWS)
    assert n_meta >= halo and n_meta <= V7X_LANES
    t = batch * seq
    lambda_init = 0.8 - 0.6 * math.exp(-0.3 * 1)

    row = lambda v: v.reshape(1, -1).astype(F32)
    pool_wb = pool_w[0].astype(BF16)
    wg, wu, wqkv = ffn_w_gate.astype(F32), ffn_w_up.astype(F32), w_qkv.astype(F32)
    wd, wo = ffn_w_down[0].astype(BF16), w_o[0].astype(BF16)
    ewg, ewu, ewd = (w.reshape(w.shape[1:]).astype(F32) for w in (exp_w_gate, exp_w_up, exp_w_down))
    r_hi = router[0].astype(BF16)
    r_lo = (router[0].astype(F32) - r_hi.astype(F32)).astype(BF16)
    router_split = jnp.pad(jnp.concatenate([r_hi, r_lo], axis=1), ((0, 0), (0, V7X_LANES - 2 * n_exp)))

    def layer0_and_qkv(tokens, first_rows, pos0):
        b, l, _ = tokens.shape
        h1, u2 = _pool_layer(tokens, first_rows, row(norm_mix[0]), pool_wb, row(pool_scale[0]),
                             row(norm_ffn[0]), pos0)
        a = _gateup(u2.reshape(b * l, d), wg, wu)
        h2, h2g, h2_ss = _matmul_residual(a, wd, h1.reshape(b * l, d), 3584, next_gain=row(norm_mix[1]))
        qkv = _qkv_proj(h2g, h2_ss, wqkv, row(q_norm[0]), row(k_norm[0]))
        return h2, qkv

    meta = meta_tokens.astype(F32)
    _, qkv_meta = layer0_and_qkv(meta[None], jnp.zeros((halo, d), F32), 0)
    qkv_meta = jnp.pad(qkv_meta, ((0, V7X_LANES - n_meta), (0, 0)))
    h2, qkv = layer0_and_qkv(x, meta[n_meta - halo:], n_meta)

    logit_bound = 1.01 * math.sqrt(HEAD_DIM) * jnp.max(jnp.abs(q_norm[0])) * jnp.max(jnp.abs(k_norm[0]))
    o = _diff_attention(qkv, qkv_meta, logit_bound.astype(F32), row(lambda_q1[0]), row(lambda_k1[0]),
                        row(lambda_q2[0]), row(lambda_k2[0]), row(subln[0]), batch, n_meta, lambda_init)
    h3 = _matmul_residual(o, wo, h2, d)

    up, route = _router(h3, row(norm_ffn[1]), router_split, n_exp)
    tm = _pick(t, 512, BF16_SUBLANES)
    n_data_tiles = (TOP_K * t) // tm + n_exp
    pos, fill, tile_expert, n_used = _routing_tables(route, n_exp, tm, n_data_tiles)
    bm_rows = _pick(t, 512, 8)
    n_steps = t // bm_rows
    assert (n_exp * tm) % n_steps == 0
    pos_tiles = pos.reshape(n_steps, 1, TOP_K * bm_rows)
    xs = _dispatch(up, pos_tiles, fill.reshape(n_steps, 1, -1), n_data_tiles * tm)
    runs = _runs_from_tile_expert(tile_expert, n_used, n_exp)
    a = _staged_matmul(xs, (ewg, ewu), runs, tm, _pick(ewg.shape[2], 512, V7X_LANES), True, "expert_gateup", "swiglu")
    y_block = _pick(d, 1024, 2 * V7X_LANES)
    y = _staged_matmul(a, (ewd,), runs, tm, y_block, False, "expert_down", "pack")
    out = _combine(h3, route, y, y_block, pos_tiles)
    return out.reshape(batch, seq, d)
```

```python
import functools
import math

import jax
import jax.numpy as jnp
from jax import lax
from jax.experimental import pallas as pl
from jax.experimental.pallas import tpu as pltpu

POOL_WINDOWS = (2, 4, 8, 16)
HEAD_DIM = 128
TOP_K = 2
RMS_EPS = 1e-6
SUBLN_EPS = 1e-5
MASK_VALUE = -1e30

V7X_LANES = 128
V7X_VMEM_BYTES = 64 * 1024 * 1024
V7X_VMEM_RESERVE_BYTES = 6 * 1024 * 1024
VMEM_TEMPORARIES_BYTES = 16 * 1024 * 1024
BF16_SUBLANES = 16
F32_SUBLANES = 8

F32 = jnp.float32
BF16 = jnp.bfloat16


def _pick(dim, target, mult):
    for b in range(min(dim, target), 0, -1):
        if dim % b == 0 and b % mult == 0:
            return b
    return dim


def _nbytes(shape, dtype):
    return math.prod(shape) * jnp.dtype(dtype).itemsize


def _params(semantics, pipelined, scratch=()):
    est = 2 * sum(_nbytes(s, d) for s, d in pipelined) + sum(_nbytes(s, d) for s, d in scratch)
    limit = min(V7X_VMEM_BYTES - V7X_VMEM_RESERVE_BYTES, est + VMEM_TEMPORARIES_BYTES)
    return pltpu.CompilerParams(dimension_semantics=semantics, vmem_limit_bytes=limit)


def _rms(v, eps):
    return v * lax.rsqrt(jnp.mean(v * v, axis=-1, keepdims=True) + eps)


def _pack_bf16_pairs(v):
    half = v.shape[1] // 2
    bits = lax.bitcast_convert_type(v.astype(BF16).astype(F32), jnp.uint32)
    return (bits[:, :half] >> 16) | (bits[:, half:] & jnp.uint32(0xFFFF0000))


def _unpack_bf16_pairs(w):
    return (lax.bitcast_convert_type(w << 16, F32),
            lax.bitcast_convert_type(w & jnp.uint32(0xFFFF0000), F32))


def _pool_kernel(x_ref, prev_ref, first_ref, gm_ref, pw_ref, ps_ref, gf_ref, h_ref, u_ref, ext_ref, lv_ref,
                 *, windows, halo, pos0):
    i = pl.program_id(1)
    tm = x_ref.shape[1]
    c = pw_ref.shape[1]
    pad = ext_ref.shape[0] - halo - tm
    top = pad + halo
    n = halo + tm
    x = x_ref[0]
    gm = gm_ref[...]
    before = jnp.where(i == 0, first_ref[...], prev_ref[0])
    ext_ref[0:pad] = jnp.zeros((pad, ext_ref.shape[1]), F32)
    lv_ref[:, 0:pad] = jnp.zeros((2, pad, c), F32)
    ext_ref[pad:top] = _rms(before, RMS_EPS) * gm
    ext_ref[top:] = _rms(x, RMS_EPS) * gm
    pos = pos0 + i * tm + lax.broadcasted_iota(jnp.int32, (tm, 1), 0)
    for g, w in enumerate(windows):
        cs = slice(g * c, (g + 1) * c)
        lv_ref[0, pad:] = ext_ref[pad:, cs] + ext_ref[pad - 1:pad - 1 + n, cs]
        cur, half = 0, 2
        while half < w:
            lv_ref[1 - cur, pad:] = lv_ref[cur, pad:] + lv_ref[cur, pad - half:pad - half + n]
            cur, half = 1 - cur, 2 * half
        inv_cnt = 1.0 / jnp.minimum(pos + 1, w).astype(F32)
        pooled = lv_ref[cur, top:] * inv_cnt - ext_ref[top:, cs]
        out = jnp.dot(pooled.astype(BF16), pw_ref[g], preferred_element_type=F32)
        h_ref[0, :, cs] = x[:, cs] + out * ps_ref[:, cs]
    h = h_ref[0]
    u_ref[0] = (_rms(h, RMS_EPS) * gf_ref[...]).astype(BF16)


def _pool_layer(x, first_rows, g_mix, pool_w, pool_scale, g_ffn, pos0):
    b, l, d = x.shape
    halo = max(POOL_WINDOWS)
    assert first_rows.shape == (halo, d)
    tm = _pick(l, 256, halo)
    nb = tm // halo
    g, c, _ = pool_w.shape
    kern = functools.partial(_pool_kernel, windows=POOL_WINDOWS, halo=halo, pos0=pos0)
    row = lambda bi, i: (bi, i, 0)
    const2 = lambda bi, i: (0, 0)
    blocks = [((1, tm, d), F32), ((1, halo, d), F32), ((halo, d), F32), ((g, c, c), BF16),
              ((1, tm, d), F32), ((1, tm, d), BF16)]
    assert all(w >= 2 and w & (w - 1) == 0 for w in POOL_WINDOWS)
    rows = F32_SUBLANES + halo + tm
    scratch = [((rows, d), F32), ((2, rows, c), F32)]
    return pl.pallas_call(
        kern,
        grid=(b, l // tm),
        in_specs=[
            pl.BlockSpec((1, tm, d), row),
            pl.BlockSpec((1, halo, d), lambda bi, i: (bi, jnp.maximum(i * nb - 1, 0), 0)),
            pl.BlockSpec((halo, d), const2),
            pl.BlockSpec((1, d), const2),
            pl.BlockSpec((g, c, c), lambda bi, i: (0, 0, 0)),
            pl.BlockSpec((1, d), const2),
            pl.BlockSpec((1, d), const2),
        ],
        out_specs=[pl.BlockSpec((1, tm, d), row), pl.BlockSpec((1, tm, d), row)],
        out_shape=[jax.ShapeDtypeStruct((b, l, d), F32), jax.ShapeDtypeStruct((b, l, d), BF16)],
        scratch_shapes=[pltpu.VMEM(s, t) for s, t in scratch],
        compiler_params=_params(("arbitrary", "arbitrary"), blocks, scratch),
        name="pool_layer",
    )(x, x, first_rows, g_mix, pool_w, pool_scale, g_ffn)


def _runs_from_tile_expert(tile_expert, n_used, n_exp):
    te = tile_expert
    prev = jnp.concatenate([te[:1] - 1, te[:-1]])
    start = (te != prev).astype(jnp.int32)
    ids = jnp.arange(n_exp, dtype=jnp.int32)
    present = jnp.any(te[None, :] == ids[:, None], axis=1)
    later = jnp.logical_and(present[None, :], ids[None, :] > ids[:, None])
    next_of = jnp.min(jnp.where(later, ids[None, :], n_exp), axis=1)
    next_of = jnp.where(next_of == n_exp, te[0], next_of).astype(jnp.int32)
    return te, n_used, start, next_of[te], (te == te[-1]).astype(jnp.int32)


def _qkv_epilogue(j, x, wb_ref, ss_ref, qn_ref, kn_ref, o_ref, *, n_q_blocks, hd):
    row_scale = lax.rsqrt(jnp.sum(ss_ref[...], axis=1, keepdims=True) * (1.0 / x.shape[1]) + RMS_EPS)
    is_qk = j < 2 * n_q_blocks
    gain = jnp.where(j < n_q_blocks, qn_ref[...] * hd ** -0.5, kn_ref[...])
    for c in range(o_ref.shape[1] // (2 * hd)):
        acc = jnp.dot(x, wb_ref[0, :, c * 2 * hd:(c + 1) * 2 * hd], preferred_element_type=F32) * row_scale
        for mp in range(2):
            blk = acc[:, mp * hd:(mp + 1) * hd]
            cs = slice((2 * c + mp) * hd, (2 * c + mp + 1) * hd)
            o_ref[:, cs] = jnp.where(is_qk, _rms(blk, RMS_EPS) * gain, blk).astype(o_ref.dtype)


def _staged_kernel(te_ref, nu_ref, st_ref, nx_ref, lr_ref, x_ref, *rest, n_w, n_extra, packed, epilogue):
    w_hbm = rest[:n_w]
    extras = rest[n_w:n_w + n_extra]
    o_ref, stage_ref, wb_ref, sem = rest[n_w + n_extra:]
    j = pl.program_id(0)
    i = pl.program_id(1)
    n_stage, _, _, bn = stage_ref.shape
    used = i < nu_ref[0]

    def copies(e, jj, slot):
        c0 = pl.multiple_of(jj * bn, bn)
        return [pltpu.make_async_copy(w.at[e, :, pl.ds(c0, bn)], stage_ref.at[slot, n], sem.at[slot, n])
                for n, w in enumerate(w_hbm)]

    @pl.when(jnp.logical_and(used, st_ref[i] == 1))
    def _():
        slot = j % n_stage
        j_next = j + lr_ref[i]

        def request_next():
            @pl.when(j_next < pl.num_programs(0))
            def _():
                for c in copies(nx_ref[i], j_next, j_next % n_stage):
                    c.start()

        @pl.when(jnp.logical_and(j == 0, i == 0))
        def _():
            for c in copies(te_ref[0], 0, 0):
                c.start()

        if n_stage == 2:
            request_next()
        for c in copies(te_ref[i], j, slot):
            c.wait()
        wb_ref[...] = stage_ref[slot].astype(BF16)
        if n_stage == 1:
            request_next()

    @pl.when(used)
    def _():
        if packed:
            d2 = x_ref.shape[1]
            lo, hi = _unpack_bf16_pairs(x_ref[...])
            parts = [(lo.astype(BF16), slice(0, d2)), (hi.astype(BF16), slice(d2, 2 * d2))]
        else:
            parts = [(x_ref[...], slice(None))]
        if callable(epilogue):
            epilogue(j, parts[0][0], wb_ref, *extras, o_ref)
            return
        prods = [sum(jnp.dot(xp, wb_ref[n, ks], preferred_element_type=F32) for xp, ks in parts)
                 for n in range(n_w)]
        if epilogue == "swiglu":
            g, u = prods
            o_ref[...] = (g * jax.nn.sigmoid(g) * u).astype(o_ref.dtype)
        else:
            o_ref[...] = _pack_bf16_pairs(prods[0])

    @pl.when(jnp.logical_not(used))
    def _():
        o_ref[...] = jnp.zeros_like(o_ref)


def _staged_matmul(x, weights, runs, tm, bn, packed, name, epilogue, extras=(), extra_specs=(), extra_blocks=(),
                   single_run=False):
    rows = x.shape[0]
    _, k, n = weights[0].shape
    n_w = len(weights)
    kern = functools.partial(_staged_kernel, n_w=n_w, n_extra=len(extras), packed=packed, epilogue=epilogue)
    out_cols, out_bn, out_dtype = (n // 2, bn // 2, jnp.uint32) if epilogue == "pack" else (n, bn, BF16)
    blocks = [((tm, x.shape[1]), x.dtype), ((tm, out_bn), out_dtype), *extra_blocks]
    stage, cast = ((n_w, k, bn), F32), ((n_w, k, bn), BF16)
    fits_two = (2 * sum(_nbytes(*b) for b in blocks) + 2 * _nbytes(*stage) + _nbytes(*cast)
                + VMEM_TEMPORARIES_BYTES <= V7X_VMEM_BYTES - V7X_VMEM_RESERVE_BYTES)
    n_stage = 2 if (single_run and fits_two) else 1
    scratch = [((n_stage, n_w, k, bn), F32), cast]
    grid_spec = pltpu.PrefetchScalarGridSpec(
        num_scalar_prefetch=5,
        grid=(n // bn, rows // tm),
        in_specs=[pl.BlockSpec((tm, x.shape[1]), lambda j, i, te, nu, *_: (jnp.minimum(i, nu[0] - 1), 0))]
        + [pl.BlockSpec(memory_space=pl.ANY)] * n_w + list(extra_specs),
        out_specs=pl.BlockSpec((tm, out_bn), lambda j, i, *_: (i, j)),
        scratch_shapes=[pltpu.VMEM(s, t) for s, t in scratch] + [pltpu.SemaphoreType.DMA((n_stage, n_w))],
    )
    return pl.pallas_call(
        kern,
        grid_spec=grid_spec,
        out_shape=jax.ShapeDtypeStruct((rows, out_cols), out_dtype),
        compiler_params=_params(("arbitrary", "arbitrary"), blocks, scratch),
        name=name,
    )(*runs, x, *weights, *extras)


def _single_run(m, bm):
    n_tiles = m // bm
    return _runs_from_tile_expert(jnp.zeros((n_tiles,), jnp.int32), jnp.full((1,), n_tiles, jnp.int32), 1)


def _gateup(x, wg, wu):
    m = x.shape[0]
    bm = _pick(m, 1024, BF16_SUBLANES)
    bn = _pick(wg.shape[2], 512, V7X_LANES)
    return _staged_matmul(x, (wg, wu), _single_run(m, bm), bm, bn, False, "ffn_gateup", "swiglu", single_run=True)


def _matmul_residual_kernel(a_ref, w_ref, r_ref, *rest, for_next_norm):
    if for_next_norm:
        g_ref, o_ref, og_ref, ss_ref = rest
    else:
        (o_ref,) = rest
    j = pl.program_id(1)
    kk = pl.program_id(2)
    part = jnp.dot(a_ref[...], w_ref[...], preferred_element_type=F32)

    @pl.when(kk == 0)
    def _():
        o_ref[...] = r_ref[...] + part

    @pl.when(kk != 0)
    def _():
        o_ref[...] += part

    if for_next_norm:
        @pl.when(kk == pl.num_programs(2) - 1)
        def _():
            o = o_ref[...]
            og_ref[...] = (o * g_ref[...]).astype(og_ref.dtype)
            sq = o * o
            lane_tiles = [sq[:, c * V7X_LANES:(c + 1) * V7X_LANES] for c in range(sq.shape[1] // V7X_LANES)]
            partial = functools.reduce(lambda x, y: x + y, lane_tiles)

            @pl.when(j == 0)
            def _():
                ss_ref[...] = partial

            @pl.when(j != 0)
            def _():
                ss_ref[...] += partial


def _matmul_residual(a, w, r, bk_target, next_gain=None):
    m, k = a.shape
    n = w.shape[1]
    for_next_norm = next_gain is not None
    bm = _pick(m, 1024, BF16_SUBLANES)
    bn = _pick(n, 1024, V7X_LANES)
    bk = _pick(k, bk_target, V7X_LANES)
    blocks = [((bm, bk), BF16), ((bk, bn), BF16), ((bm, bn), F32), ((bm, bn), F32)]
    in_specs = [
        pl.BlockSpec((bm, bk), lambda i, j, kk: (i, kk)),
        pl.BlockSpec((bk, bn), lambda i, j, kk: (kk, j)),
        pl.BlockSpec((bm, bn), lambda i, j, kk: (i, j)),
    ]
    out_specs = [pl.BlockSpec((bm, bn), lambda i, j, kk: (i, j))]
    out_shape = [jax.ShapeDtypeStruct((m, n), F32)]
    operands = [a, w, r]
    if for_next_norm:
        blocks += [((bm, bn), BF16), ((bm, V7X_LANES), F32)]
        in_specs.append(pl.BlockSpec((1, bn), lambda i, j, kk: (0, j)))
        out_specs += [pl.BlockSpec((bm, bn), lambda i, j, kk: (i, j)),
                      pl.BlockSpec((bm, V7X_LANES), lambda i, j, kk: (i, 0))]
        out_shape += [jax.ShapeDtypeStruct((m, n), BF16), jax.ShapeDtypeStruct((m, V7X_LANES), F32)]
        operands.append(next_gain)
    out = pl.pallas_call(
        functools.partial(_matmul_residual_kernel, for_next_norm=for_next_norm),
        grid=(m // bm, n // bn, k // bk),
        in_specs=in_specs,
        out_specs=out_specs,
        out_shape=out_shape,
        compiler_params=_params(("arbitrary", "arbitrary", "arbitrary"), blocks),
        name="matmul_residual",
    )(*operands)
    return out if for_next_norm else out[0]


def _qkv_proj(hg, ss, w, q_norm, k_norm):
    m, d = hg.shape
    bm = _pick(m, 1024, BF16_SUBLANES)
    bn = _pick(d, 1024, 2 * HEAD_DIM)
    head = pl.BlockSpec((1, HEAD_DIM), lambda j, i, *_: (0, 0))
    return _staged_matmul(
        hg, (w,), _single_run(m, bm), bm, bn, False, "qkv_proj",
        functools.partial(_qkv_epilogue, n_q_blocks=d // bn, hd=HEAD_DIM),
        extras=(ss, q_norm, k_norm),
        extra_specs=(pl.BlockSpec((bm, V7X_LANES), lambda j, i, *_: (i, 0)), head, head),
        extra_blocks=(((bm, V7X_LANES), F32),),
        single_run=True,
    )


def _attn_kernel(q_ref, k_ref, v_ref, km_ref, vm_ref, lq1_ref, lk1_ref, lq2_ref, lk2_ref, sub_ref,
                 o_ref, acc_ref, m_ref, l_ref, *, tk, hd, n_meta, lambda_init):
    qi = pl.program_id(2)
    tq = q_ref.shape[0]
    nt = (((1,), (1,)), ((), ()))

    def update(mp, k, v, mask, first):
        q = q_ref[:, mp * hd:(mp + 1) * hd]
        s = lax.dot_general(q, k, nt, preferred_element_type=F32)
        if mask is not None:
            s = jnp.where(mask, s, MASK_VALUE)
        row_max = jnp.max(s, axis=1, keepdims=True)
        if first:
            m_new = row_max
        else:
            m_old = m_ref[mp][:, :1]
            m_new = jnp.maximum(m_old, row_max)
        p = jnp.exp(s - m_new)
        row_sum = jnp.sum(p, axis=1, keepdims=True)
        pv = jnp.dot(p.astype(BF16), v, preferred_element_type=F32)
        if first:
            l_new = row_sum
            acc_ref[mp] = pv
        else:
            alpha = jnp.exp(m_old - m_new)
            l_new = alpha * l_ref[mp][:, :1] + row_sum
            acc_ref[mp] = alpha * acc_ref[mp] + pv
        m_ref[mp] = jnp.broadcast_to(m_new, (tq, V7X_LANES))
        l_ref[mp] = jnp.broadcast_to(l_new, (tq, V7X_LANES))

    meta_mask = lax.broadcasted_iota(jnp.int32, (tq, km_ref.shape[0]), 1) < n_meta
    for mp in range(2):
        update(mp, km_ref[:, mp * hd:(mp + 1) * hd], vm_ref[...], meta_mask, True)

    n_full = (qi * tq) // tk

    def full_tile(kt, carry):
        k0 = pl.multiple_of(kt * tk, tk)
        for mp in range(2):
            update(mp, k_ref[pl.ds(k0, tk), mp * hd:(mp + 1) * hd], v_ref[pl.ds(k0, tk), :], None, False)
        return carry

    lax.fori_loop(0, n_full, full_tile, 0)

    row = lax.broadcasted_iota(jnp.int32, (tq, tk), 0)
    col = lax.broadcasted_iota(jnp.int32, (tq, tk), 1)
    for dt in range(tq // tk):
        k0 = pl.multiple_of(qi * tq + dt * tk, tk)
        diag_mask = col + dt * tk <= row
        for mp in range(2):
            update(mp, k_ref[pl.ds(k0, tk), mp * hd:(mp + 1) * hd], v_ref[pl.ds(k0, tk), :], diag_mask, False)

    lam = (jnp.exp(jnp.sum(lq1_ref[...] * lk1_ref[...], axis=-1, keepdims=True))
           - jnp.exp(jnp.sum(lq2_ref[...] * lk2_ref[...], axis=-1, keepdims=True)) + lambda_init)
    o1 = acc_ref[0] / l_ref[0][:, :1]
    o2 = acc_ref[1] / l_ref[1][:, :1]
    o = o1 - lam * o2
    o = _rms(o, SUBLN_EPS) * sub_ref[...] * (1.0 - lambda_init)
    o_ref[...] = o.astype(o_ref.dtype)


def _attn_bounded_kernel(shift_ref, q_ref, k_ref, v_ref, km_ref, vm_ref, lq1_ref, lk1_ref, lq2_ref, lk2_ref,
                         sub_ref, o_ref, kt_ref, acc_ref, ls_ref, *, hd, n_meta, lambda_init):
    qi = pl.program_id(2)
    tq = q_ref.shape[0]
    n_kt, _, tk = kt_ref.shape
    shift = shift_ref[0, 0]

    @pl.when(qi == 0)
    def _():
        def transpose_tile(c, carry):
            r0 = pl.multiple_of(c * tk, tk)
            kt_ref[c] = k_ref[pl.ds(r0, tk), :].T
            return carry

        lax.fori_loop(0, n_kt, transpose_tile, 0)

    all_rows = slice(0, tq)

    def accumulate(mp, rows, s, v, mask, first):
        p = jnp.exp(s - shift)
        if mask is not None:
            p = jnp.where(mask, p, 0.0)
        part = p[:, 0:V7X_LANES]
        for c in range(1, p.shape[1] // V7X_LANES):
            part = part + p[:, c * V7X_LANES:(c + 1) * V7X_LANES]
        pv = jnp.dot(p.astype(BF16), v, preferred_element_type=F32)
        if first:
            ls_ref[mp, rows] = part
            acc_ref[mp, rows] = pv
        else:
            ls_ref[mp, rows] += part
            acc_ref[mp, rows] += pv

    def kv_tile(kt, rows, mask, first=False):
        k0 = pl.multiple_of(kt * tk, tk)
        v = v_ref[pl.ds(k0, tk), :]
        for mp in range(2):
            hs = slice(mp * hd, (mp + 1) * hd)
            s = jnp.dot(q_ref[rows, hs], kt_ref[kt, hs, :], preferred_element_type=F32)
            accumulate(mp, rows, s, v, mask, first)

    def causal(n_rows):
        row = lax.broadcasted_iota(jnp.int32, (n_rows, tk), 0)
        col = lax.broadcasted_iota(jnp.int32, (n_rows, tk), 1)
        return col <= row

    kv_tile(2 * qi, all_rows, causal(tq), first=True)

    def full_tiles(first, count):
        for t in range(count):
            kv_tile(first + t, all_rows, None)

    def four_full_tiles(quad, carry):
        full_tiles(4 * quad, 4)
        return carry

    lax.fori_loop(0, qi // 2, four_full_tiles, 0)

    @pl.when(qi % 2 == 1)
    def _():
        full_tiles(2 * qi - 2, 2)

    nt = (((1,), (1,)), ((), ()))
    meta_mask = lax.broadcasted_iota(jnp.int32, (tq, km_ref.shape[0]), 1) < n_meta
    for mp in range(2):
        hs = slice(mp * hd, (mp + 1) * hd)
        s = lax.dot_general(q_ref[:, hs], km_ref[:, hs], nt, preferred_element_type=F32)
        accumulate(mp, all_rows, s, vm_ref[...], meta_mask, False)
    kv_tile(2 * qi + 1, slice(tk, tq), causal(tq - tk))

    lam = (jnp.exp(jnp.sum(lq1_ref[...] * lk1_ref[...], axis=-1, keepdims=True))
           - jnp.exp(jnp.sum(lq2_ref[...] * lk2_ref[...], axis=-1, keepdims=True)) + lambda_init)
    o1 = acc_ref[0] / jnp.sum(ls_ref[0], axis=1, keepdims=True)
    o2 = acc_ref[1] / jnp.sum(ls_ref[1], axis=1, keepdims=True)
    o = o1 - lam * o2
    o = _rms(o, SUBLN_EPS) * sub_ref[...] * (1.0 - lambda_init)
    o_ref[...] = o.astype(o_ref.dtype)


SAFE_LOGIT_BOUND = 32.0


def _diff_attention(qkv, qkv_meta, logit_bound, lq1, lk1, lq2, lk2, subln, batch, n_meta, lambda_init):
    t, n3 = qkv.shape
    d = n3 // 3
    l = t // batch
    hw = 2 * HEAD_DIM
    heads = d // hw
    tq = _pick(l, 1024, 2 * V7X_LANES)
    tk = tq // 2
    nq = l // tq
    mrows = qkv_meta.shape[0]
    vec = pl.BlockSpec((1, HEAD_DIM), lambda b, h, i: (0, 0))
    blocks = [((tq, hw), BF16), ((l, hw), BF16), ((l, hw), BF16), ((mrows, hw), BF16), ((mrows, hw), BF16),
              ((tq, hw), BF16)]
    in_specs = [
        pl.BlockSpec((tq, hw), lambda b, h, i: (b * nq + i, h)),
        pl.BlockSpec((l, hw), lambda b, h, i: (b, heads + h)),
        pl.BlockSpec((l, hw), lambda b, h, i: (b, 2 * heads + h)),
        pl.BlockSpec((mrows, hw), lambda b, h, i: (0, heads + h)),
        pl.BlockSpec((mrows, hw), lambda b, h, i: (0, 2 * heads + h)),
        vec, vec, vec, vec,
        pl.BlockSpec((1, hw), lambda b, h, i: (0, 0)),
    ]
    common = dict(
        grid=(batch, heads, nq),
        out_specs=pl.BlockSpec((tq, hw), lambda b, h, i: (b * nq + i, h)),
        out_shape=jax.ShapeDtypeStruct((t, d), BF16),
    )
    operands = (qkv, qkv, qkv, qkv_meta, qkv_meta, lq1, lk1, lq2, lk2, subln)

    def running_max(_):
        kern = functools.partial(_attn_kernel, tk=tk, hd=HEAD_DIM, n_meta=n_meta, lambda_init=lambda_init)
        scratch = [((2, tq, hw), F32), ((2, tq, V7X_LANES), F32), ((2, tq, V7X_LANES), F32)]
        return pl.pallas_call(
            kern, in_specs=in_specs,
            scratch_shapes=[pltpu.VMEM(s, ty) for s, ty in scratch],
            compiler_params=_params(("arbitrary", "arbitrary", "arbitrary"), blocks, scratch),
            name="diff_attention_running_max", **common,
        )(*operands)

    def bounded(shift):
        kern = functools.partial(_attn_bounded_kernel, hd=HEAD_DIM, n_meta=n_meta, lambda_init=lambda_init)
        scratch = [((l // tk, hw, tk), BF16), ((2, tq, hw), F32), ((2, tq, V7X_LANES), F32)]
        return pl.pallas_call(
            kern,
            in_specs=[pl.BlockSpec((1, 1), lambda b, h, i: (0, 0), memory_space=pltpu.SMEM)] + in_specs,
            scratch_shapes=[pltpu.VMEM(s, ty) for s, ty in scratch],
            compiler_params=_params(("arbitrary", "arbitrary", "arbitrary"), blocks, scratch),
            name="diff_attention_bounded", **common,
        )(shift, *operands)

    shift = logit_bound.reshape(1, 1).astype(F32)
    return lax.cond(logit_bound <= SAFE_LOGIT_BOUND, bounded, running_max, shift)


def _router_kernel(x_ref, g_ref, r_ref, up_ref, rt_ref, *, n_exp):
    x = x_ref[...]
    u = _rms(x, RMS_EPS) * g_ref[...]
    u_hi = u.astype(BF16)
    u_lo = (u - u_hi.astype(F32)).astype(BF16)
    both = (jnp.dot(u_hi, r_ref[...], preferred_element_type=F32)
            + jnp.dot(u_lo, r_ref[...], preferred_element_type=F32))
    logits = both + pltpu.roll(both, V7X_LANES - n_exp, 1)
    lane = lax.broadcasted_iota(jnp.int32, logits.shape, 1).astype(F32)
    lg = jnp.where(lane < n_exp, logits, -jnp.inf)
    m1 = jnp.max(lg, axis=1, keepdims=True)
    i1 = jnp.min(jnp.where(lg == m1, lane, float(V7X_LANES)), axis=1, keepdims=True)
    lg2 = jnp.where(lane == i1, -jnp.inf, lg)
    m2 = jnp.max(lg2, axis=1, keepdims=True)
    i2 = jnp.min(jnp.where(lg2 == m2, lane, float(V7X_LANES)), axis=1, keepdims=True)
    e = jnp.exp(m2 - m1)
    g1 = 1.0 / (1.0 + e)
    g2 = e / (1.0 + e)
    rt = jnp.where(lane == 0, i1, jnp.where(lane == 1, i2, jnp.where(lane == 2, g1, jnp.where(lane == 3, g2, 0.0))))
    rt_ref[...] = rt
    up_ref[...] = _pack_bf16_pairs(u)


def _router(h, g, router_split, n_exp):
    m, d = h.shape
    bm = _pick(m, 512, 8)
    kern = functools.partial(_router_kernel, n_exp=n_exp)
    blocks = [((bm, d), F32), ((d, V7X_LANES), BF16), ((bm, d // 2), jnp.uint32), ((bm, V7X_LANES), F32)]
    return pl.pallas_call(
        kern,
        grid=(m // bm,),
        in_specs=[
            pl.BlockSpec((bm, d), lambda i: (i, 0)),
            pl.BlockSpec((1, d), lambda i: (0, 0)),
            pl.BlockSpec((d, V7X_LANES), lambda i: (0, 0)),
        ],
        out_specs=[pl.BlockSpec((bm, d // 2), lambda i: (i, 0)), pl.BlockSpec((bm, V7X_LANES), lambda i: (i, 0))],
        out_shape=[jax.ShapeDtypeStruct((m, d // 2), jnp.uint32), jax.ShapeDtypeStruct((m, V7X_LANES), F32)],
        compiler_params=_params(("arbitrary",), blocks),
        name="router",
    )(h, g, router_split)


ROW_DMA_UNROLL = 8


def _dispatch_kernel(pos_ref, fill_ref, u_ref, xs_ref, sem):
    bm = u_ref.shape[0]
    n_fill = fill_ref.shape[2]

    def row_copy(r, s):
        return pltpu.make_async_copy(u_ref.at[pl.ds(r, 1)], xs_ref.at[pl.ds(pos_ref[0, 0, 2 * r + s], 1)], sem)

    def fill_copy(k):
        return pltpu.make_async_copy(u_ref.at[pl.ds(k, 1)], xs_ref.at[pl.ds(fill_ref[0, 0, k], 1)], sem)

    def start(r, c):
        row_copy(r, 0).start(priority=0)
        row_copy(r, 1).start(priority=1)
        return c

    def wait(r, c):
        row_copy(r, 0).wait()
        row_copy(r, 1).wait()
        return c

    lax.fori_loop(0, bm, start, 0, unroll=ROW_DMA_UNROLL)
    lax.fori_loop(0, n_fill, lambda k, c: (fill_copy(k).start(), c)[1], 0, unroll=ROW_DMA_UNROLL)
    lax.fori_loop(0, bm, wait, 0, unroll=ROW_DMA_UNROLL)
    lax.fori_loop(0, n_fill, lambda k, c: (fill_copy(k).wait(), c)[1], 0, unroll=ROW_DMA_UNROLL)


def _dispatch(up, pos_tiles, fill_tiles, n_rows):
    m, d2 = up.shape
    bm = pos_tiles.shape[2] // TOP_K
    n_fill = fill_tiles.shape[2]
    assert fill_tiles.shape[0] == m // bm and n_fill <= bm
    blocks = [((bm, d2), jnp.uint32)]
    return pl.pallas_call(
        _dispatch_kernel,
        grid=(m // bm,),
        in_specs=[
            pl.BlockSpec((1, 1, TOP_K * bm), lambda i: (i, 0, 0), memory_space=pltpu.SMEM),
            pl.BlockSpec((1, 1, n_fill), lambda i: (i, 0, 0), memory_space=pltpu.SMEM),
            pl.BlockSpec((bm, d2), lambda i: (i, 0)),
        ],
        out_specs=pl.BlockSpec(memory_space=pl.ANY),
        out_shape=jax.ShapeDtypeStruct((n_rows, d2), jnp.uint32),
        scratch_shapes=[pltpu.SemaphoreType.DMA(())],
        compiler_params=_params(("arbitrary",), blocks),
        name="moe_dispatch",
    )(pos_tiles, fill_tiles, up)


def _combine_kernel(pos_ref, pos_next_ref, h_ref, rt_ref, y_ref, o_ref, buf_ref, sem, *, y_block):
    i = pl.program_id(0)
    bm = h_ref.shape[0]
    slot = i % 2

    def row_copy(p_ref, sl, r, s):
        return pltpu.make_async_copy(y_ref.at[pl.ds(p_ref[0, 0, 2 * r + s], 1)],
                                     buf_ref.at[sl, s, pl.ds(r, 1)], sem.at[sl])

    def start(p_ref, sl, r):
        row_copy(p_ref, sl, r, 0).start(priority=0)
        row_copy(p_ref, sl, r, 1).start(priority=1)

    def wait_all(p_ref, sl):
        def wait(r, c):
            row_copy(p_ref, sl, r, 0).wait()
            row_copy(p_ref, sl, r, 1).wait()
            return c

        lax.fori_loop(0, bm, wait, 0, unroll=ROW_DMA_UNROLL)

    @pl.when(i == 0)
    def _():
        lax.fori_loop(0, bm, lambda r, c: (start(pos_ref, 0, r), c)[1], 0, unroll=ROW_DMA_UNROLL)

    wait_all(pos_ref, slot)
    for r in range(bm):
        start(pos_next_ref, 1 - slot, r)
    rt = rt_ref[...]
    g1, g2 = rt[:, 2:3], rt[:, 3:4]
    half = y_block // 2
    for blk in range(h_ref.shape[1] // y_block):
        words = slice(blk * half, (blk + 1) * half)
        lo1, hi1 = _unpack_bf16_pairs(buf_ref[slot, 0, :, words])
        lo2, hi2 = _unpack_bf16_pairs(buf_ref[slot, 1, :, words])
        c_lo = slice(blk * y_block, blk * y_block + half)
        c_hi = slice(blk * y_block + half, (blk + 1) * y_block)
        o_ref[:, c_lo] = h_ref[:, c_lo] + g1 * lo1 + g2 * lo2
        o_ref[:, c_hi] = h_ref[:, c_hi] + g1 * hi1 + g2 * hi2

    @pl.when(i == pl.num_programs(0) - 1)
    def _():
        wait_all(pos_next_ref, 1 - slot)


def _combine(h, route, y, y_block, pos_tiles):
    m, d = h.shape
    n_tiles = pos_tiles.shape[0]
    bm = pos_tiles.shape[2] // TOP_K
    blocks = [((bm, d), F32), ((bm, V7X_LANES), F32), ((bm, d), F32)]
    scratch = [((2, TOP_K, bm, d // 2), jnp.uint32)]
    pos_spec = lambda index: pl.BlockSpec((1, 1, TOP_K * bm), index, memory_space=pltpu.SMEM)
    return pl.pallas_call(
        functools.partial(_combine_kernel, y_block=y_block),
        grid=(n_tiles,),
        in_specs=[
            pos_spec(lambda i: (i, 0, 0)),
            pos_spec(lambda i: (jnp.minimum(i + 1, n_tiles - 1), 0, 0)),
            pl.BlockSpec((bm, d), lambda i: (i, 0)),
            pl.BlockSpec((bm, V7X_LANES), lambda i: (i, 0)),
            pl.BlockSpec(memory_space=pl.ANY),
        ],
        out_specs=pl.BlockSpec((bm, d), lambda i: (i, 0)),
        out_shape=jax.ShapeDtypeStruct((m, d), F32),
        scratch_shapes=[pltpu.VMEM(*scratch[0]), pltpu.SemaphoreType.DMA((2,))],
        compiler_params=_params(("arbitrary",), blocks, scratch),
        name="moe_combine",
    )(pos_tiles, pos_tiles, h, route, y)


def _routing_tables(route, n_exp, tm, n_data_tiles):
    experts = route[:, :TOP_K].astype(jnp.int32).reshape(-1)
    ids = jnp.arange(n_exp, dtype=jnp.int32)
    onehot = (experts[:, None] == ids[None, :]).astype(jnp.int32)
    csum = jnp.cumsum(onehot, axis=0)
    rank = jnp.sum(csum * onehot, axis=1) - 1
    counts = csum[-1]
    padded = ((counts + tm - 1) // tm) * tm
    ends = jnp.cumsum(padded)
    starts = ends - padded
    pos = jnp.sum(starts[None, :] * onehot, axis=1) + rank
    pads = padded - counts
    pad_ends = jnp.cumsum(pads)
    f = jnp.arange(n_data_tiles * tm - experts.shape[0], dtype=jnp.int32)
    owner = jnp.sum((f[:, None] >= pad_ends[None, :]).astype(jnp.int32), axis=1)
    e = jnp.minimum(owner, n_exp - 1)
    in_padding = (starts + counts)[e] + f - (pad_ends - pads)[e]
    past_last = ends[-1] + f - pad_ends[-1]
    fill = jnp.where(owner < n_exp, in_padding, past_last)
    n_used = (ends[-1] // tm).astype(jnp.int32)
    tile_row = jnp.minimum(jnp.arange(n_data_tiles, dtype=jnp.int32), n_used - 1) * tm
    tile_expert = jnp.sum((tile_row[:, None] >= ends[None, :]).astype(jnp.int32), axis=1)
    return pos, fill, jnp.minimum(tile_expert, n_exp - 1).astype(jnp.int32), n_used.reshape(1)


def kernel(x, meta_tokens, norm_mix, norm_ffn, pool_w, pool_scale, ffn_w_gate, ffn_w_up, ffn_w_down,
           w_qkv, q_norm, k_norm, lambda_q1, lambda_k1, lambda_q2, lambda_k2, subln, w_o,
           router, exp_w_gate, exp_w_up, exp_w_down):
    batch, seq, d = x.shape
    n_meta = meta_tokens.shape[0]
    n_exp = router.shape[-1]
    assert norm_mix.shape[0] == 2, "one pooling layer followed by one attention layer"
    halo = max(POOL_WINDOWS)
    assert n_meta >= halo and n_meta <= V7X_LANES
    t = batch * seq
    lambda_init = 0.8 - 0.6 * math.exp(-0.3 * 1)

    row = lambda v: v.reshape(1, -1).astype(F32)
    pool_wb = pool_w[0].astype(BF16)
    wg, wu, wqkv = ffn_w_gate.astype(F32), ffn_w_up.astype(F32), w_qkv.astype(F32)
    wd, wo = ffn_w_down[0].astype(BF16), w_o[0].astype(BF16)
    ewg, ewu, ewd = (w.reshape(w.shape[1:]).astype(F32) for w in (exp_w_gate, exp_w_up, exp_w_down))
    r_hi = router[0].astype(BF16)
    r_lo = (router[0].astype(F32) - r_hi.astype(F32)).astype(BF16)
    router_split = jnp.pad(jnp.concatenate([r_hi, r_lo], axis=1), ((0, 0), (0, V7X_LANES - 2 * n_exp)))

    def layer0_and_qkv(tokens, first_rows, pos0):
        b, l, _ = tokens.shape
        h1, u2 = _pool_layer(tokens, first_rows, row(norm_mix[0]), pool_wb, row(pool_scale[0]),
                             row(norm_ffn[0]), pos0)
        a = _gateup(u2.reshape(b * l, d), wg, wu)
        h2, h2g, h2_ss = _matmul_residual(a, wd, h1.reshape(b * l, d), 3584, next_gain=row(norm_mix[1]))
        qkv = _qkv_proj(h2g, h2_ss, wqkv, row(q_norm[0]), row(k_norm[0]))
        return h2, qkv

    meta = meta_tokens.astype(F32)
    _, qkv_meta = layer0_and_qkv(meta[None], jnp.zeros((halo, d), F32), 0)
    qkv_meta = jnp.pad(qkv_meta, ((0, V7X_LANES - n_meta), (0, 0)))
    h2, qkv = layer0_and_qkv(x, meta[n_meta - halo:], n_meta)

    logit_bound = 1.01 * math.sqrt(HEAD_DIM) * jnp.max(jnp.abs(q_norm[0])) * jnp.max(jnp.abs(k_norm[0]))
    o = _diff_attention(qkv, qkv_meta, logit_bound.astype(F32), row(lambda_q1[0]), row(lambda_k1[0]),
                        row(lambda_q2[0]), row(lambda_k2[0]), row(subln[0]), batch, n_meta, lambda_init)
    h3 = _matmul_residual(o, wo, h2, d)

    up, route = _router(h3, row(norm_ffn[1]), router_split, n_exp)
    tm = _pick(t, 512, BF16_SUBLANES)
    n_data_tiles = (TOP_K * t) // tm + n_exp
    pos, fill, tile_expert, n_used = _routing_tables(route, n_exp, tm, n_data_tiles)
    dispatch_rows, combine_rows = _pick(t, 512, 8), _pick(t, 256, 8)
    n_steps = t // dispatch_rows
    assert (n_exp * tm) % n_steps == 0
    xs = _dispatch(up, pos.reshape(n_steps, 1, TOP_K * dispatch_rows), fill.reshape(n_steps, 1, -1),
                   n_data_tiles * tm)
    runs = _runs_from_tile_expert(tile_expert, n_used, n_exp)
    a = _staged_matmul(xs, (ewg, ewu), runs, tm, _pick(ewg.shape[2], 512, V7X_LANES), True, "expert_gateup", "swiglu")
    y_block = _pick(d, 1024, 2 * V7X_LANES)
    y = _staged_matmul(a, (ewd,), runs, tm, y_block, False, "expert_down", "pack")
    out = _combine(h3, route, y, y_block, pos.reshape(t // combine_rows, 1, TOP_K * combine_rows))
    return out.reshape(batch, seq, d)
```
